```python
import math
import jax, jax.numpy as jnp
from jax import lax
import numpy as np

D_MODEL = 2048
BATCH = 8
SEQ = 2048
DEPTH = 1

CHUNK = 64
N_META = 16
E_CONV = D_MODEL // 2
E_SSM = D_MODEL // 2
CONV_WIDTH = 31
SSM_GROUP = 16
SSM_STATE = 64
N_SSM_GROUPS = E_SSM // SSM_GROUP
NORM_EPS = 1e-6
LN_EPS = 1e-5
DT_MIN = 1e-3
DT_MAX = 1e-1
IN_SPLITS = (E_CONV, E_CONV, E_CONV, E_SSM, E_SSM, D_MODEL, D_MODEL)
IN_WIDTH = sum(IN_SPLITS)

kernel_name = "gated_conformer_s5_hybrid_block"


def _rmsnorm(x, gain):
    x32 = x.astype(jnp.float32)
    y = x32 * lax.rsqrt(jnp.mean(x32 * x32, axis=-1, keepdims=True) + NORM_EPS)
    return (y * gain.astype(jnp.float32)).astype(x.dtype)


def _layernorm(x, gain, bias):
    x32 = x.astype(jnp.float32)
    mu = jnp.mean(x32, axis=-1, keepdims=True)
    var = jnp.mean(jnp.square(x32 - mu), axis=-1, keepdims=True)
    y = (x32 - mu) * lax.rsqrt(var + LN_EPS)
    return (y * gain.astype(jnp.float32) + bias.astype(jnp.float32)).astype(x.dtype)


def _split_columns(p):
    idx = np.cumsum(IN_SPLITS)[:-1].tolist()
    return jnp.split(p, idx, axis=-1)


def _conformer_conv(v, g, z, dw_w, dw_b, ln_g, ln_b):
    a = v * jax.nn.sigmoid(g)
    a = lax.conv_general_dilated(
        a, dw_w.astype(a.dtype), window_strides=(1,),
        padding=[(CONV_WIDTH - 1, 0)],
        dimension_numbers=("NWC", "WIO", "NWC"),
        feature_group_count=E_CONV) + dw_b
    a = _layernorm(a, ln_g, ln_b)
    a = jax.nn.silu(a)
    return a * jax.nn.silu(z)


def _s5_scan(u, lam_re, lam_im, log_step, b_re, b_im, c_re, c_im, d_skip):
    bsz, L, _ = u.shape
    u32 = u.astype(jnp.float32).reshape(bsz, L, N_SSM_GROUPS, SSM_GROUP)
    lam = lax.complex(jnp.minimum(lam_re.astype(jnp.float32), -1e-4),
                      lam_im.astype(jnp.float32))
    step = jnp.exp(log_step.astype(jnp.float32))[:, None]
    lam_bar = jnp.exp(lam * step)
    b_mat = lax.complex(b_re.astype(jnp.float32), b_im.astype(jnp.float32))
    b_bar = ((lam_bar - 1.0) / lam)[..., None] * b_mat
    bu = jnp.einsum("gph,blgh->blgp", b_bar, u32.astype(jnp.complex64))
    a = jnp.broadcast_to(lam_bar, bu.shape)

    def combine(left, right):
        a_l, b_l = left
        a_r, b_r = right
        return a_r * a_l, a_r * b_l + b_r

    _, states = lax.associative_scan(combine, (a, bu), axis=1)
    c_mat = lax.complex(c_re.astype(jnp.float32), c_im.astype(jnp.float32))
    y = jnp.real(jnp.einsum("ghp,blgp->blgh", c_mat, states))
    y = y + d_skip.astype(jnp.float32).reshape(N_SSM_GROUPS, SSM_GROUP) * u32
    return y.reshape(bsz, L, E_SSM).astype(u.dtype)


def setup_inputs(seed: int = 0) -> dict:
    key = jax.random.key(seed)
    ks = jax.random.split(key, 24)
    f32 = jnp.float32
    nrm = lambda k, shape, scale: jax.random.normal(k, shape, f32) * scale
    n_idx = jnp.arange(SSM_STATE, dtype=f32)
    lam_re = -0.5 + 0.01 * jax.random.normal(ks[10], (N_SSM_GROUPS, SSM_STATE), f32)
    lam_im = math.pi * n_idx[None, :] + 0.01 * jax.random.normal(ks[11], (N_SSM_GROUPS, SSM_STATE), f32)
    log_step = jax.random.uniform(ks[12], (N_SSM_GROUPS,), f32,
                                  math.log(DT_MIN), math.log(DT_MAX))
    return {
        "x": nrm(ks[0], (BATCH, SEQ, D_MODEL), 1.0),
        "meta": nrm(ks[1], (N_META, D_MODEL), 1.0),
        "norm_g": 1.0 + nrm(ks[2], (D_MODEL,), 0.01),
        "w_in": nrm(ks[3], (D_MODEL, IN_WIDTH), D_MODEL ** -0.5),
        "b_gate": nrm(ks[4], (2 * D_MODEL,), 0.01),
        "dw_w": nrm(ks[5], (CONV_WIDTH, 1, E_CONV), CONV_WIDTH ** -0.5),
        "dw_b": nrm(ks[6], (E_CONV,), 0.01),
        "ln_g": 1.0 + nrm(ks[7], (E_CONV,), 0.01),
        "ln_b": nrm(ks[8], (E_CONV,), 0.01),
        "w_conv": nrm(ks[9], (E_CONV, D_MODEL), E_CONV ** -0.5),
        "lam_re": lam_re,
        "lam_im": lam_im,
        "log_step": log_step,
        "b_re": nrm(ks[13], (N_SSM_GROUPS, SSM_STATE, SSM_GROUP), (2 * SSM_GROUP) ** -0.5),
        "b_im": nrm(ks[14], (N_SSM_GROUPS, SSM_STATE, SSM_GROUP), (2 * SSM_GROUP) ** -0.5),
        "c_re": nrm(ks[15], (N_SSM_GROUPS, SSM_GROUP, SSM_STATE), (2 * SSM_STATE) ** -0.5),
        "c_im": nrm(ks[16], (N_SSM_GROUPS, SSM_GROUP, SSM_STATE), (2 * SSM_STATE) ** -0.5),
        "d_skip": nrm(ks[17], (E_SSM,), 1.0),
        "w_glu": nrm(ks[18], (E_SSM, E_SSM), E_SSM ** -0.5),
        "b_glu": nrm(ks[19], (E_SSM,), 0.01),
        "w_ssm": nrm(ks[20], (E_SSM, D_MODEL), E_SSM ** -0.5),
        "w_out": nrm(ks[21], (D_MODEL, D_MODEL), D_MODEL ** -0.5),
        "final_g": 1.0 + nrm(ks[22], (D_MODEL,), 0.01),
    }


def reference(x, meta, norm_g, w_in, b_gate, dw_w, dw_b, ln_g, ln_b, w_conv,
              lam_re, lam_im, log_step, b_re, b_im, c_re, c_im, d_skip,
              w_glu, b_glu, w_ssm, w_out, final_g):
    bsz = x.shape[0]
    h = jnp.concatenate(
        [jnp.broadcast_to(meta[None].astype(x.dtype), (bsz, N_META, D_MODEL)), x], axis=1)
    for _ in range(DEPTH):
        xn = _rmsnorm(h, norm_g)
        proj = jnp.einsum("bld,de->ble", xn, w_in)
        v_c, g_c, z_c, u_s, z_s, gl_c, gl_s = _split_columns(proj)
        y_c = _conformer_conv(v_c, g_c, z_c, dw_w, dw_b, ln_g, ln_b)
        y_s = _s5_scan(u_s, lam_re, lam_im, log_step, b_re, b_im, c_re, c_im, d_skip)
        y_s = jax.nn.gelu(y_s)
        y_s = y_s * jax.nn.sigmoid(jnp.einsum("ble,ef->blf", y_s, w_glu) + b_glu)
        y_s = y_s * jax.nn.silu(z_s)
        gate_c = jax.nn.sigmoid(gl_c + b_gate[:D_MODEL])
        gate_s = jax.nn.sigmoid(gl_s + b_gate[D_MODEL:])
        merged = (gate_c * jnp.einsum("ble,ed->bld", y_c, w_conv)
                  + gate_s * jnp.einsum("ble,ed->bld", y_s, w_ssm))
        h = h + jnp.einsum("bld,de->ble", merged, w_out)
    out = _rmsnorm(h, final_g)
    return out[:, N_META:, :]
```

```python
import functools
import math

import jax
import jax.numpy as jnp
from jax import lax
from jax.experimental import pallas as pl
from jax.experimental.pallas import tpu as pltpu

F32 = jnp.float32
BF16 = jnp.bfloat16

T_CHUNK = 16
LANES_V7X = 128
NORM_EPS = 1e-6
LN_EPS = 1e-5
LAM_RE_MAX = -1e-4
VMEM_LIMIT_V7X = 56 * 1024 * 1024


def _cparams(*sem):
    return pltpu.CompilerParams(dimension_semantics=sem, vmem_limit_bytes=VMEM_LIMIT_V7X)


def _sigmoid(x):
    return jax.nn.sigmoid(x)


def _silu(x):
    return x * _sigmoid(x)


def _dot(a, b):
    return jnp.dot(a, b, preferred_element_type=F32)


def _dot_nt(a, b):
    return lax.dot_general(a, b, (((1,), (1,)), ((), ())), preferred_element_type=F32)


def _dot_tn(a, b):
    return lax.dot_general(a, b, (((0,), (0,)), ((), ())), preferred_element_type=F32)


def _norm_slab(x_ref, gain_ref, xn_ref):
    nb, nc, _ = x_ref.shape
    gain = gain_ref[...]

    def body(b, carry):
        x = x_ref[b]
        ms = jnp.mean(x * x, axis=-1, keepdims=True)
        r0 = pl.multiple_of(b * nc, nc)
        xn_ref[pl.ds(r0, nc), :] = (x * lax.rsqrt(ms + NORM_EPS) * gain).astype(BF16)
        return carry

    lax.fori_loop(0, nb, body, 0)


def _proj_conv_kernel(x_ref, gain_ref, wv_ref, wg_ref, wz_ref, a_ref, zc_ref, xn_ref):
    @pl.when(pl.program_id(1) == 0)
    def _():
        _norm_slab(x_ref, gain_ref, xn_ref)

    xn = xn_ref[...]
    v = _dot(xn, wv_ref[...])
    g = _dot(xn, wg_ref[...])
    z = _dot(xn, wz_ref[...])
    a_ref[...] = v * _sigmoid(g)
    zc_ref[...] = _silu(z).astype(BF16)


def _proj_gate_kernel(x_ref, gain_ref, w_ref, b_ref, o_ref, xn_ref):
    @pl.when(pl.program_id(1) == 0)
    def _():
        _norm_slab(x_ref, gain_ref, xn_ref)

    o_ref[...] = _sigmoid(_dot(xn_ref[...], w_ref[...]) + b_ref[...]).astype(BF16)


def _proj_u_kernel(x_ref, gain_ref, wt_ref, ut_ref, xn_ref):
    @pl.when(pl.program_id(1) == 0)
    def _():
        _norm_slab(x_ref, gain_ref, xn_ref)

    acc = _dot_nt(wt_ref[...], xn_ref[...])
    ut_ref[...] = acc.reshape(ut_ref.shape)


def _proj_zs_kernel(x_ref, gain_ref, wt_ref, zt_ref, xn_ref):
    @pl.when(pl.program_id(1) == 0)
    def _():
        _norm_slab(x_ref, gain_ref, xn_ref)

    zt_ref[...] = _silu(_dot_nt(wt_ref[...], xn_ref[...])).astype(BF16)


def _proj_meta_kernel(m_ref, gain_ref, wv_ref, wg_ref, wu_ref, a_ref, u_ref):
    x = m_ref[...]
    ms = jnp.mean(x * x, axis=-1, keepdims=True)
    xn = (x * lax.rsqrt(ms + NORM_EPS) * gain_ref[...]).astype(BF16)
    a_ref[...] = _dot(xn, wv_ref[...]) * _sigmoid(_dot(xn, wg_ref[...]))
    u_ref[...] = _dot(xn, wu_ref[...])


def _cmul(ar, ai, br, bi):
    return ar * br - ai * bi, ar * bi + ai * br


def _lam_bar(lr, li, dt):
    lr = jnp.minimum(lr, LAM_RE_MAX)
    mag = jnp.exp(lr * dt)
    th = li * dt
    return lr, mag * jnp.cos(th), mag * jnp.sin(th)


def _s5_param_kernel(lrc_ref, lic_ref, lrr_ref, lir_ref, ls_ref, btr_ref, bti_ref,
                     ca_ref, cb_ref, m_ref, r_ref, o_ref, ap_ref, *, n_scan):
    p = lrc_ref.shape[0]
    t = T_CHUNK
    h = btr_ref.shape[1] // t
    dt = jnp.exp(ls_ref[...])

    lr, lam_r, lam_i = _lam_bar(lrc_ref[...], lic_ref[...], dt)
    li = lic_ref[...]
    den = lr * lr + li * li
    nr, ni = lam_r - 1.0, lam_i
    kr = (nr * lr + ni * li) / den
    ki = (ni * lr - nr * li) / den
    bbr, bbi = _cmul(kr, ki, btr_ref[...], bti_ref[...])

    lane = lax.broadcasted_iota(jnp.int32, (p, t * h), 1)
    expo = (t - 1) - lane // h
    pr = jnp.ones((p, t * h), F32)
    pi = jnp.zeros((p, t * h), F32)
    br_, bi_ = lam_r, lam_i
    for k in range(int(math.log2(t))):
        qr, qi = _cmul(pr, pi, br_, bi_)
        bit = ((expo >> k) & 1) == 1
        pr = jnp.where(bit, qr, pr)
        pi = jnp.where(bit, qi, pi)
        br_, bi_ = _cmul(br_, bi_, br_, bi_)
    wr, wi = _cmul(pr, pi, bbr, bbi)
    r_ref[0:p, :] = wr
    r_ref[p:2 * p, :] = wi

    lane16 = lax.broadcasted_iota(jnp.int32, ap_ref.shape, 1)
    ap = jnp.zeros(ap_ref.shape, F32)
    ar, ai = br_, bi_
    for k in range(n_scan):
        ap = jnp.where(lane16 == 2 * k, ar, ap)
        ap = jnp.where(lane16 == 2 * k + 1, ai, ap)
        ar, ai = _cmul(ar, ai, ar, ai)
    ap_ref[...] = ap

    ca = ca_ref[...]
    cb = cb_ref[...]
    sgn = jnp.where(lax.broadcasted_iota(jnp.int32, ca.shape, 1) < p, 1.0, -1.0)
    w = jnp.concatenate([wr, wi], axis=0)
    krev = jnp.dot(sgn * ca, w, preferred_element_type=F32, precision=lax.Precision.HIGHEST)

    half = LANES_V7X
    ka, kb = krev[:, :half], krev[:, half:]
    lane_h = lax.broadcasted_iota(jnp.int32, (h, half), 1)
    zero = jnp.zeros((h, half), F32)
    for i in range(t):
        s = (t - 1 - i) * h
        if s == 0:
            lo, hi = ka, kb
        elif s < half:
            ra = pltpu.roll(ka, half - s, 1)
            rb = pltpu.roll(kb, half - s, 1)
            keep = lane_h < (half - s)
            lo, hi = jnp.where(keep, ra, rb), jnp.where(keep, rb, zero)
        elif s == half:
            lo, hi = kb, zero
        else:
            rb = pltpu.roll(kb, 2 * half - s, 1)
            lo, hi = jnp.where(lane_h < (2 * half - s), rb, zero), zero
        m_ref[i * h:(i + 1) * h, 0:half] = lo
        m_ref[i * h:(i + 1) * h, half:2 * half] = hi

    _, l1r, l1i = _lam_bar(lrr_ref[...], lir_ref[...], dt)
    qr, qi = l1r, l1i
    for i in range(t):
        o_ref[i * h:(i + 1) * h, :] = sgn * (ca * qr) - cb * qi
        qr, qi = _cmul(qr, qi, l1r, l1i)


def _roll_chunks(x, shift):
    nv = x.shape[1] // LANES_V7X
    parts = [pltpu.roll(x[:, v * LANES_V7X:(v + 1) * LANES_V7X], shift, 1) for v in range(nv)]
    return jnp.concatenate(parts, axis=1)


def _s5_scan_kernel(ut_ref, um_ref, m_ref, r_ref, o_ref, ap_ref, dcol_ref, yt_ref, *, n_scan):
    p = ap_ref.shape[0]
    u = ut_ref[...]
    ub = u.astype(BF16)
    r32 = r_ref[...]
    yt = _dot(m_ref[...].astype(BF16), ub)
    zt = _dot(r32.astype(BF16), ub)

    s0 = jnp.sum(r32 * um_ref[...], axis=1, keepdims=True)
    s0r, s0i = s0[:p], s0[p:]
    ap = ap_ref[...]
    c_idx = lax.broadcasted_iota(jnp.int32, (p, u.shape[1]), 1) % LANES_V7X
    first = c_idx == 0
    ir, ii = _cmul(ap[:, 0:1], ap[:, 1:2], s0r, s0i)
    er = zt[:p] + jnp.where(first, ir, 0.0)
    ei = zt[p:] + jnp.where(first, ii, 0.0)
    for k in range(n_scan):
        sh = 1 << k
        rr, ri = _roll_chunks(er, sh), _roll_chunks(ei, sh)
        dr, di = _cmul(ap[:, 2 * k:2 * k + 1], ap[:, 2 * k + 1:2 * k + 2], rr, ri)
        ok = c_idx >= sh
        er = er + jnp.where(ok, dr, 0.0)
        ei = ei + jnp.where(ok, di, 0.0)
    sr = jnp.where(first, s0r, _roll_chunks(er, 1))
    si = jnp.where(first, s0i, _roll_chunks(ei, 1))
    s_in = jnp.concatenate([sr, si], axis=0).astype(BF16)
    yt = yt + _dot(o_ref[...].astype(BF16), s_in) + dcol_ref[...] * u
    yt_ref[...] = yt.reshape(yt_ref.shape)


def _glu_kernel(yt_ref, zt_ref, wt_ref, bcol_ref, o_ref):
    y = jax.nn.gelu(yt_ref[...], approximate=True)
    g = _dot(wt_ref[...], y.astype(BF16)) + bcol_ref[...]
    o_ref[...] = (y * _sigmoid(g) * zt_ref[...].astype(F32)).astype(BF16)


def _conv_kernel(a_ref, am_ref, w_ref, b_ref, o_ref, s1_ref, s2_ref, *, width):
    t, nc, cw = a_ref.shape
    zrow = jnp.zeros((1, cw), F32)
    for j in range(t):
        mrow = am_ref[j:j + 1, :]
        s1_ref[j, 0:1, :] = mrow
        s1_ref[j, 1:nc, :] = a_ref[j, 0:nc - 1, :]
        s2_ref[j, 0:1, :] = zrow
        s2_ref[j, 1:2, :] = mrow
        s2_ref[j, 2:nc, :] = a_ref[j, 0:nc - 2, :]
    srcs = (a_ref, s1_ref, s2_ref)
    bias = b_ref[...]
    for jo in range(t):
        acc = jnp.broadcast_to(bias, (nc, cw))
        for s in range(width):
            r, q = s % t, s // t
            src = srcs[q + (1 if r > jo else 0)]
            acc = acc + w_ref[width - 1 - s:width - s, :] * src[(jo - r) % t]
        o_ref[jo] = acc


def _merge_kernel(cv_ref, zc_ref, gt_ref, yt_ref, lng_ref, lnb_ref, wc_ref, ws_ref, o_ref):
    d = o_ref.shape[-1]
    cv = cv_ref[...]
    mu = jnp.mean(cv, axis=-1, keepdims=True)
    xc = cv - mu
    var = jnp.mean(xc * xc, axis=-1, keepdims=True)
    yn = xc * lax.rsqrt(var + LN_EPS) * lng_ref[...] + lnb_ref[...]
    yc = (_silu(yn) * zc_ref[...].astype(F32)).astype(BF16)
    pc = _dot(yc, wc_ref[...])
    ps = _dot_tn(yt_ref[...], ws_ref[...])
    o_ref[...] = (gt_ref[:, 0:d].astype(F32) * pc + gt_ref[:, d:2 * d].astype(F32) * ps).astype(BF16)


def _out_kernel(mg_ref, x_ref, w_ref, gain_ref, o_ref):
    nb, nc, d = x_ref.shape
    hres = x_ref[...].reshape(nb * nc, d) + _dot(mg_ref[...], w_ref[...])
    ms = jnp.mean(hres * hres, axis=-1, keepdims=True)
    o_ref[...] = (hres * lax.rsqrt(ms + NORM_EPS) * gain_ref[...]).reshape(nb, nc, d)


def _col_tile(n):
    return 512 if n % 512 == 0 else n


def kernel(x, meta, norm_g, w_in, b_gate, dw_w, dw_b, ln_g, ln_b, w_conv, lam_re, lam_im,
           log_step, b_re, b_im, c_re, c_im, d_skip, w_glu, b_glu, w_ssm, w_out, final_g):
    bsz, seq, d = x.shape
    t = T_CHUNK
    e = w_conv.shape[0]
    g_n, p_n = lam_re.shape
    h_n = e // g_n
    width = dw_w.shape[0]
    nc = seq // t
    rows = bsz * nc
    assert meta.shape[0] == t and h_n * t == 2 * LANES_V7X and nc == LANES_V7X
    assert width <= 2 * t - 1 and seq % t == 0
    n_scan = int(math.log2(nc))
    tn = _col_tile(e)
    ne = e // tn

    xv = x.reshape(bsz, nc, t * d)
    w_in_b = w_in.astype(BF16)
    wt_us = w_in[:, 3 * e:5 * e].T.astype(BF16)
    gain = norm_g.reshape(1, d)
    x_spec = pl.BlockSpec((bsz, nc, d), lambda j, n: (0, 0, j))
    gain_spec = pl.BlockSpec((1, d), lambda j, n: (0, 0))
    xn_scratch = pltpu.VMEM((rows, d), BF16)

    def wcol(off):
        return pl.BlockSpec((d, tn), lambda j, n, off=off: (0, off + n))

    a_slab, zc_slab = pl.pallas_call(
        _proj_conv_kernel,
        grid=(t, ne),
        in_specs=[x_spec, gain_spec, wcol(0), wcol(ne), wcol(2 * ne)],
        out_specs=[pl.BlockSpec((None, rows, tn), lambda j, n: (j, 0, n)),
                   pl.BlockSpec((None, rows, tn), lambda j, n: (j, 0, n))],
        out_shape=[jax.ShapeDtypeStruct((t, rows, e), F32),
                   jax.ShapeDtypeStruct((t, rows, e), BF16)],
        scratch_shapes=[xn_scratch],
        compiler_params=_cparams("parallel", "arbitrary"),
        name="proj_conv",
    )(xv, gain, w_in_b, w_in_b, w_in_b)

    ng = (2 * d) // tn
    gates = pl.pallas_call(
        _proj_gate_kernel,
        grid=(t, ng),
        in_specs=[x_spec, gain_spec, wcol(5 * ne),
                  pl.BlockSpec((1, tn), lambda j, n: (0, n))],
        out_specs=pl.BlockSpec((None, rows, tn), lambda j, n: (j, 0, n)),
        out_shape=jax.ShapeDtypeStruct((t, rows, 2 * d), BF16),
        scratch_shapes=[xn_scratch],
        compiler_params=_cparams("parallel", "arbitrary"),
        name="proj_gate",
    )(xv, gain, w_in_b, b_gate.reshape(1, 2 * d))

    gb = 256 // h_n
    ut = pl.pallas_call(
        _proj_u_kernel,
        grid=(t, g_n // gb),
        in_specs=[x_spec, gain_spec, pl.BlockSpec((gb * h_n, d), lambda j, n: (n, 0))],
        out_specs=pl.BlockSpec((gb, h_n, rows), lambda j, n: (n, j, 0)),
        out_shape=jax.ShapeDtypeStruct((g_n, t * h_n, rows), F32),
        scratch_shapes=[xn_scratch],
        compiler_params=_cparams("parallel", "arbitrary"),
        name="proj_u",
    )(xv, gain, wt_us)

    zst = pl.pallas_call(
        _proj_zs_kernel,
        grid=(t, e // 256),
        in_specs=[x_spec, gain_spec, pl.BlockSpec((256, d), lambda j, n: (e // 256 + n, 0))],
        out_specs=pl.BlockSpec((None, 256, rows), lambda j, n: (j, n, 0)),
        out_shape=jax.ShapeDtypeStruct((t, e, rows), BF16),
        scratch_shapes=[xn_scratch],
        compiler_params=_cparams("parallel", "arbitrary"),
        name="proj_zs",
    )(xv, gain, wt_us)

    a_meta, u_meta = pl.pallas_call(
        _proj_meta_kernel,
        grid=(ne,),
        in_specs=[pl.BlockSpec((t, d), lambda n: (0, 0)), pl.BlockSpec((1, d), lambda n: (0, 0)),
                  pl.BlockSpec((d, tn), lambda n: (0, n)), pl.BlockSpec((d, tn), lambda n: (0, ne + n)),
                  pl.BlockSpec((d, tn), lambda n: (0, 3 * ne + n))],
        out_specs=[pl.BlockSpec((t, tn), lambda n: (0, n)), pl.BlockSpec((t, tn), lambda n: (0, n))],
        out_shape=[jax.ShapeDtypeStruct((t, e), F32), jax.ShapeDtypeStruct((t, e), F32)],
        compiler_params=_cparams("arbitrary"),
        name="proj_meta",
    )(meta, gain, w_in_b, w_in_b, w_in_b)
    um_flat = u_meta.reshape(t, g_n, h_n).transpose(1, 0, 2).reshape(g_n, 1, t * h_n)

    grp = lambda *shape: pl.BlockSpec((None,) + shape, lambda g: (g,) + (0,) * len(shape))
    tile2 = lambda v: jnp.concatenate([v, v], axis=-1)
    m_op, r_op, o_op, a_pow = pl.pallas_call(
        functools.partial(_s5_param_kernel, n_scan=n_scan),
        grid=(g_n,),
        in_specs=[grp(p_n, 1), grp(p_n, 1), grp(1, 2 * p_n), grp(1, 2 * p_n), grp(1, 1),
                  grp(p_n, t * h_n), grp(p_n, t * h_n), grp(h_n, 2 * p_n), grp(h_n, 2 * p_n)],
        out_specs=[grp(t * h_n, t * h_n), grp(2 * p_n, t * h_n), grp(t * h_n, 2 * p_n), grp(p_n, 2 * n_scan + 2)],
        out_shape=[jax.ShapeDtypeStruct((g_n, t * h_n, t * h_n), F32),
                   jax.ShapeDtypeStruct((g_n, 2 * p_n, t * h_n), F32),
                   jax.ShapeDtypeStruct((g_n, t * h_n, 2 * p_n), F32),
                   jax.ShapeDtypeStruct((g_n, p_n, 2 * n_scan + 2), F32)],
        compiler_params=_cparams("parallel"),
        name="s5_params",
    )(lam_re.reshape(g_n, p_n, 1), lam_im.reshape(g_n, p_n, 1),
      tile2(lam_re).reshape(g_n, 1, 2 * p_n), tile2(lam_im).reshape(g_n, 1, 2 * p_n),
      log_step.reshape(g_n, 1, 1),
      jnp.tile(b_re, (1, 1, t)), jnp.tile(b_im, (1, 1, t)),
      jnp.concatenate([c_re, c_im], axis=-1), jnp.concatenate([c_im, c_re], axis=-1))

    dcol = jnp.tile(d_skip.reshape(g_n, 1, h_n), (1, t, 1)).reshape(g_n, t * h_n, 1)
    yt = pl.pallas_call(
        functools.partial(_s5_scan_kernel, n_scan=n_scan),
        grid=(g_n,),
        in_specs=[grp(t * h_n, rows), grp(1, t * h_n), grp(t * h_n, t * h_n), grp(2 * p_n, t * h_n),
                  grp(t * h_n, 2 * p_n), grp(p_n, 2 * n_scan + 2), grp(t * h_n, 1)],
        out_specs=pl.BlockSpec((t, h_n, rows), lambda g: (0, g, 0)),
        out_shape=jax.ShapeDtypeStruct((t, e, rows), F32),
        compiler_params=_cparams("parallel"),
        name="s5_scan",
    )(ut, um_flat, m_op, r_op, o_op, a_pow, dcol)

    rt = rows // 2
    y2t = pl.pallas_call(
        _glu_kernel,
        grid=(t, 2),
        in_specs=[pl.BlockSpec((None, e, rt), lambda j, r: (j, 0, r)),
                  pl.BlockSpec((None, e, rt), lambda j, r: (j, 0, r)),
                  pl.BlockSpec((e, e), lambda j, r: (0, 0)),
                  pl.BlockSpec((e, 1), lambda j, r: (0, 0))],
        out_specs=pl.BlockSpec((None, e, rt), lambda j, r: (j, 0, r)),
        out_shape=jax.ShapeDtypeStruct((t, e, rows), BF16),
        compiler_params=_cparams("parallel", "parallel"),
        name="s5_glu",
    )(yt, zst, w_glu.T.astype(BF16), b_glu.reshape(e, 1))

    cw = 256
    conv = pl.pallas_call(
        functools.partial(_conv_kernel, width=width),
        grid=(bsz, e // cw),
        in_specs=[pl.BlockSpec((t, nc, cw), lambda b, c: (0, b, c)),
                  pl.BlockSpec((t, cw), lambda b, c: (0, c)),
                  pl.BlockSpec((width, cw), lambda b, c: (0, c)),
                  pl.BlockSpec((1, cw), lambda b, c: (0, c))],
        out_specs=pl.BlockSpec((t, nc, cw), lambda b, c: (0, b, c)),
        out_shape=jax.ShapeDtypeStruct((t, rows, e), F32),
        scratch_shapes=[pltpu.VMEM((t, nc, cw), F32), pltpu.VMEM((t, nc, cw), F32)],
        compiler_params=_cparams("parallel", "parallel"),
        name="dw_conv",
    )(a_slab, a_meta, dw_w.reshape(width, e), dw_b.reshape(1, e))

    merged = pl.pallas_call(
        _merge_kernel,
        grid=(t, 2),
        in_specs=[pl.BlockSpec((None, rt, e), lambda j, r: (j, r, 0)),
                  pl.BlockSpec((None, rt, e), lambda j, r: (j, r, 0)),
                  pl.BlockSpec((None, rt, 2 * d), lambda j, r: (j, r, 0)),
                  pl.BlockSpec((None, e, rt), lambda j, r: (j, 0, r)),
                  pl.BlockSpec((1, e), lambda j, r: (0, 0)),
                  pl.BlockSpec((1, e), lambda j, r: (0, 0)),
                  pl.BlockSpec((e, d), lambda j, r: (0, 0)),
                  pl.BlockSpec((e, d), lambda j, r: (0, 0))],
        out_specs=pl.BlockSpec((None, rt, d), lambda j, r: (j, r, 0)),
        out_shape=jax.ShapeDtypeStruct((t, rows, d), BF16),
        compiler_params=_cparams("parallel", "parallel"),
        name="merge",
    )(conv, zc_slab, gates, y2t, ln_g.reshape(1, e), ln_b.reshape(1, e),
      w_conv.astype(BF16), w_ssm.astype(BF16))

    hb = bsz // 2
    out = pl.pallas_call(
        _out_kernel,
        grid=(t, 2),
        in_specs=[pl.BlockSpec((None, rt, d), lambda j, r: (j, r, 0)),
                  pl.BlockSpec((hb, nc, d), lambda j, r: (r, 0, j)),
                  pl.BlockSpec((d, d), lambda j, r: (0, 0)),
                  pl.BlockSpec((1, d), lambda j, r: (0, 0))],
        out_specs=pl.BlockSpec((hb, nc, d), lambda j, r: (r, 0, j)),
        out_shape=jax.ShapeDtypeStruct((bsz, nc, t * d), F32),
        compiler_params=_cparams("parallel", "parallel"),
        name="out_proj",
    )(merged, xv, w_out.astype(BF16), final_g.reshape(1, d))
    return out.reshape(bsz, seq, d)
```

```python
import functools
import math

import jax
import jax.numpy as jnp
from jax import lax
from jax.experimental import pallas as pl
from jax.experimental.pallas import tpu as pltpu

F32 = jnp.float32
BF16 = jnp.bfloat16

T_CHUNK = 16
LANES_V7X = 128
SUBLANES_V7X = 8
NORM_EPS = 1e-6
LN_EPS = 1e-5
LAM_RE_MAX = -1e-4
VMEM_LIMIT_V7X = 56 * 1024 * 1024


def _cparams(*sem):
    return pltpu.CompilerParams(dimension_semantics=sem, vmem_limit_bytes=VMEM_LIMIT_V7X)


def _sigmoid(x):
    return 0.5 * jnp.tanh(0.5 * x) + 0.5


def _silu(x):
    return x * _sigmoid(x)


def _dot(a, b):
    return jnp.dot(a, b, preferred_element_type=F32)


def _dot_nt(a, b):
    return lax.dot_general(a, b, (((1,), (1,)), ((), ())), preferred_element_type=F32)


def _dot_tn(a, b):
    return lax.dot_general(a, b, (((0,), (0,)), ((), ())), preferred_element_type=F32)


def _dot_tt(a, b):
    return lax.dot_general(a, b, (((0,), (1,)), ((), ())), preferred_element_type=F32)


def _cmul(ar, ai, br, bi):
    return ar * br - ai * bi, ar * bi + ai * br


def _norm_slab(x_ref, gain_ref, xn_ref):
    nb, nc, _ = x_ref.shape
    gain = gain_ref[...]

    def body(b, carry):
        x = x_ref[b]
        ms = jnp.mean(x * x, axis=-1, keepdims=True)
        r0 = pl.multiple_of(b * nc, nc)
        xn_ref[pl.ds(r0, nc), :] = (x * lax.rsqrt(ms + NORM_EPS) * gain).astype(BF16)
        return carry

    lax.fori_loop(0, nb, body, 0)


def _proj_kernel(x_hbm, gain_ref, wv_ref, wg_ref, wz_ref, wgt_ref, bg_ref, wt_ref,
                 a_ref, zc_ref, gt_ref, ut_ref, zt_ref, xbuf, xn_ref, sem,
                 *, n_conv, n_gate, n_u):
    j, n = pl.program_id(0), pl.program_id(1)

    def slab_copy(jj):
        return pltpu.make_async_copy(x_hbm.at[:, :, jj, :], xbuf, sem.at[0])

    @pl.when(n == 0)
    def _():
        @pl.when(j == 0)
        def _():
            slab_copy(0).start()

        slab_copy(j).wait()
        _norm_slab(xbuf, gain_ref, xn_ref)

        @pl.when(j + 1 < pl.num_programs(0))
        def _():
            slab_copy(j + 1).start()

    @pl.when(n < n_conv)
    def _():
        xn = xn_ref[...]
        v = _dot(xn, wv_ref[...])
        g = _dot(xn, wg_ref[...])
        z = _dot(xn, wz_ref[...])
        a_ref[...] = v * _sigmoid(g)
        zc_ref[...] = _silu(z).astype(BF16)

    @pl.when((n >= n_conv) & (n < n_conv + n_gate))
    def _():
        gt_ref[...] = _sigmoid(_dot(xn_ref[...], wgt_ref[...]) + bg_ref[...]).astype(BF16)

    @pl.when((n >= n_conv + n_gate) & (n < n_conv + n_gate + n_u))
    def _():
        acc = _dot_tt(wt_ref[...], xn_ref[...])
        ut_ref[...] = acc.reshape(ut_ref.shape)

    @pl.when(n >= n_conv + n_gate + n_u)
    def _():
        zt_ref[...] = _silu(_dot_tt(wt_ref[...], xn_ref[...])).astype(BF16)


def _proj_meta_kernel(m_ref, gain_ref, wv_ref, wg_ref, wu_ref, a_ref, u_ref):
    x = m_ref[...]
    ms = jnp.mean(x * x, axis=-1, keepdims=True)
    xn = (x * lax.rsqrt(ms + NORM_EPS) * gain_ref[...]).astype(BF16)
    a_ref[...] = _dot(xn, wv_ref[...]) * _sigmoid(_dot(xn, wg_ref[...]))
    u_ref[...] = _dot(xn, wu_ref[...])


def _lam_bar(lr, li, dt):
    lr = jnp.minimum(lr, LAM_RE_MAX)
    mag = jnp.exp(lr * dt)
    th = li * dt
    return lr, mag * jnp.cos(th), mag * jnp.sin(th)


def _s5_param_kernel(lrc_ref, lic_ref, lrr_ref, lir_ref, ls_ref, btr_ref, bti_ref,
                     ca_ref, cb_ref, m_ref, r_ref, o_ref, pw_ref):
    p = lrc_ref.shape[0]
    t = T_CHUNK
    h = btr_ref.shape[1] // t
    dt = jnp.exp(ls_ref[...])

    lr, lam_r, lam_i = _lam_bar(lrc_ref[...], lic_ref[...], dt)
    li = lic_ref[...]
    den = lr * lr + li * li
    nr, ni = lam_r - 1.0, lam_i
    kr = (nr * lr + ni * li) / den
    ki = (ni * lr - nr * li) / den
    bbr, bbi = _cmul(kr, ki, btr_ref[...], bti_ref[...])

    lane = lax.broadcasted_iota(jnp.int32, (p, t * h), 1)
    expo = (t - 1) - lane // h
    pr = jnp.ones((p, t * h), F32)
    pi = jnp.zeros((p, t * h), F32)
    br_, bi_ = lam_r, lam_i
    for k in range(int(math.log2(t))):
        qr, qi = _cmul(pr, pi, br_, bi_)
        bit = ((expo >> k) & 1) == 1
        pr = jnp.where(bit, qr, pr)
        pi = jnp.where(bit, qi, pi)
        br_, bi_ = _cmul(br_, bi_, br_, bi_)
    wr, wi = _cmul(pr, pi, bbr, bbi)
    r_ref[0:p, :] = wr
    r_ref[p:2 * p, :] = wi

    ca = ca_ref[...]
    cb = cb_ref[...]
    sgn = jnp.where(lax.broadcasted_iota(jnp.int32, ca.shape, 1) < p, 1.0, -1.0)
    w = jnp.concatenate([wr, wi], axis=0)
    krev = jnp.dot(sgn * ca, w, preferred_element_type=F32, precision=lax.Precision.HIGHEST)

    half = LANES_V7X
    ka, kb = krev[:, :half], krev[:, half:]
    lane_h = lax.broadcasted_iota(jnp.int32, (h, half), 1)
    zero = jnp.zeros((h, half), F32)
    for i in range(t):
        s = (t - 1 - i) * h
        if s == 0:
            lo, hi = ka, kb
        elif s < half:
            ra = pltpu.roll(ka, half - s, 1)
            rb = pltpu.roll(kb, half - s, 1)
            keep = lane_h < (half - s)
            lo, hi = jnp.where(keep, ra, rb), jnp.where(keep, rb, zero)
        elif s == half:
            lo, hi = kb, zero
        else:
            rb = pltpu.roll(kb, 2 * half - s, 1)
            lo, hi = jnp.where(lane_h < (2 * half - s), rb, zero), zero
        m_ref[i * h:(i + 1) * h, 0:half] = lo
        m_ref[i * h:(i + 1) * h, half:2 * half] = hi

    _, l1r, l1i = _lam_bar(lrr_ref[...], lir_ref[...], dt)
    qr, qi = l1r, l1i
    for i in range(t):
        o_ref[i * h:(i + 1) * h, :] = sgn * (ca * qr) - cb * qi
        qr, qi = _cmul(qr, qi, l1r, l1i)

    ar, ai = l1r, l1i
    for _ in range(int(math.log2(t))):
        ar, ai = _cmul(ar, ai, ar, ai)
    qr, qi = ar, ai
    for s in range(SUBLANES_V7X):
        pw_ref[s:s + 1, :] = qr
        pw_ref[SUBLANES_V7X + s:SUBLANES_V7X + s + 1, :] = qi
        qr, qi = _cmul(qr, qi, ar, ai)


def _s5_scan_kernel(ut_ref, um_ref, m_ref, r_ref, o_ref, pw_ref, dcol_ref, yt_ref):
    ng, th, rows = ut_ref.shape
    p = r_ref.shape[1] // 2
    lw, sl = LANES_V7X, SUBLANES_V7X
    nb = rows // lw
    h = yt_ref.shape[1] // ng
    lane0 = lax.broadcasted_iota(jnp.int32, (th, lw), 1) == 0

    ys, zre, zim = [], [], []
    for g in range(ng):
        meta_blk = jnp.where(lane0, um_ref[g], 0.0)
        ue = jnp.concatenate([ut_ref[g], meta_blk], axis=1).astype(BF16)
        ys.append(_dot(m_ref[g].astype(BF16), ue[:, :rows]))
        z = _dot(r_ref[g].astype(BF16), ue)
        zre.append(z[:p])
        zim.append(z[p:])
    zr = jnp.concatenate(zre, axis=0)
    zi = jnp.concatenate(zim, axis=0)

    def chunk_major(z):
        return jnp.concatenate([z[:, b * lw:(b + 1) * lw].T for b in range(nb)], axis=1)

    def tile_lanes(v):
        return jnp.concatenate([v] * nb, axis=1)

    er, ei = chunk_major(zr), chunk_major(zi)
    car_r = tile_lanes(zr[:, rows:rows + lw].T[0:1, :])
    car_i = tile_lanes(zi[:, rows:rows + lw].T[0:1, :])
    pw = tile_lanes(pw_ref[...])
    nc, width = er.shape
    srow = lax.broadcasted_iota(jnp.int32, (nc, width), 0) % sl

    def tile_roll(x, sh):
        return pltpu.roll(x.reshape(nc // sl, sl, width), sh, 1).reshape(nc, width)

    for sh in (1, 2, 4):
        dr, di = _cmul(pw[sh - 1:sh], pw[sl + sh - 1:sl + sh], tile_roll(er, sh), tile_roll(ei, sh))
        ok = srow >= sh
        er = er + jnp.where(ok, dr, 0.0)
        ei = ei + jnp.where(ok, di, 0.0)

    a8r, a8i = pw[0:sl], pw[sl:2 * sl]
    first = lax.broadcasted_iota(jnp.int32, (sl, width), 0) == 0
    xr_rows, xi_rows = [], []
    for r in range(nc // sl):
        cr, ci = _cmul(a8r, a8i, car_r, car_i)
        fr = er[r * sl:(r + 1) * sl] + cr
        fi = ei[r * sl:(r + 1) * sl] + ci
        xr_rows.append(jnp.where(first, car_r, pltpu.roll(fr, 1, 0)))
        xi_rows.append(jnp.where(first, car_i, pltpu.roll(fi, 1, 0)))
        car_r, car_i = fr[sl - 1:sl], fi[sl - 1:sl]
    sr = jnp.concatenate(xr_rows, axis=0)
    si = jnp.concatenate(xi_rows, axis=0)

    def state_major(s):
        return jnp.concatenate([s[:, b * lw:(b + 1) * lw].T for b in range(nb)], axis=1)

    srt, sit = state_major(sr), state_major(si)
    for g in range(ng):
        s_in = jnp.concatenate([srt[g * p:(g + 1) * p], sit[g * p:(g + 1) * p]], axis=0).astype(BF16)
        y = ys[g] + _dot(o_ref[g].astype(BF16), s_in) + dcol_ref[g] * ut_ref[g]
        yt_ref[:, g * h:(g + 1) * h, :] = y.reshape(th // h, h, rows)


def _glu_kernel(yt_ref, zt_ref, wt_ref, bcol_ref, o_ref):
    y = jax.nn.gelu(yt_ref[...], approximate=True)
    g = _dot(wt_ref[...], y.astype(BF16)) + bcol_ref[...]
    o_ref[...] = (y * _sigmoid(g) * zt_ref[...].astype(F32)).astype(BF16)


def _conv_kernel(a_ref, am_ref, w_ref, b_ref, o_ref, s1_ref, s2_ref, *, width):
    t, nc, cw = a_ref.shape
    zrow = jnp.zeros((1, cw), F32)
    for j in range(t):
        mrow = am_ref[j:j + 1, :]
        s1_ref[j, 0:1, :] = mrow
        s1_ref[j, 1:nc, :] = a_ref[j, 0:nc - 1, :]
        s2_ref[j, 0:1, :] = zrow
        s2_ref[j, 1:2, :] = mrow
        s2_ref[j, 2:nc, :] = a_ref[j, 0:nc - 2, :]
    srcs = (a_ref, s1_ref, s2_ref)
    bias = b_ref[...]
    for jo in range(t):
        acc = jnp.broadcast_to(bias, (nc, cw))
        for s in range(width):
            r, q = s % t, s // t
            src = srcs[q + (1 if r > jo else 0)]
            acc = acc + w_ref[width - 1 - s:width - s, :] * src[(jo - r) % t]
        o_ref[jo] = acc


def _merge_kernel(cv_ref, zc_ref, gt_ref, yt_ref, lng_ref, lnb_ref, wc_ref, ws_ref, o_ref):
    d = o_ref.shape[-1]
    cv = cv_ref[...]
    mu = jnp.mean(cv, axis=-1, keepdims=True)
    xc = cv - mu
    var = jnp.mean(xc * xc, axis=-1, keepdims=True)
    yn = xc * lax.rsqrt(var + LN_EPS) * lng_ref[...] + lnb_ref[...]
    yc = (_silu(yn) * zc_ref[...].astype(F32)).astype(BF16)
    pc = _dot(yc, wc_ref[...])
    ps = _dot_tn(yt_ref[...], ws_ref[...])
    o_ref[...] = (gt_ref[:, 0:d].astype(F32) * pc + gt_ref[:, d:2 * d].astype(F32) * ps).astype(BF16)


def _out_kernel(mg_ref, w_ref, gain_ref, x_hbm, o_hbm, xbuf, obuf, sem_in, sem_out):
    j, r = pl.program_id(0), pl.program_id(1)
    nr = pl.num_programs(1)
    step, n_steps = j * nr + r, pl.num_programs(0) * nr
    _, hb, nc, d = xbuf.shape
    slot = step % 2

    def x_copy(jj, rr, sl):
        return pltpu.make_async_copy(x_hbm.at[pl.ds(rr * hb, hb), :, jj, :], xbuf.at[sl], sem_in.at[sl])

    def o_copy(jj, rr, sl):
        return pltpu.make_async_copy(obuf.at[sl], o_hbm.at[pl.ds(rr * hb, hb), :, jj, :], sem_out.at[sl])

    @pl.when(step == 0)
    def _():
        x_copy(0, 0, 0).start()

    x_copy(j, r, slot).wait()

    @pl.when(step + 1 < n_steps)
    def _():
        nxt = step + 1
        x_copy(nxt // nr, nxt % nr, 1 - slot).start()

    @pl.when(step >= 2)
    def _():
        o_copy(j, r, slot).wait()

    hres = xbuf[slot].reshape(hb * nc, d) + _dot(mg_ref[...], w_ref[...])
    ms = jnp.mean(hres * hres, axis=-1, keepdims=True)
    obuf[slot] = (hres * lax.rsqrt(ms + NORM_EPS) * gain_ref[...]).reshape(hb, nc, d)
    o_copy(j, r, slot).start()

    @pl.when(step == n_steps - 1)
    def _():
        o_copy(j, r, 1 - slot).wait()
        o_copy(j, r, slot).wait()


def _col_tile(n):
    return 512 if n % 512 == 0 else n


def kernel(x, meta, norm_g, w_in, b_gate, dw_w, dw_b, ln_g, ln_b, w_conv, lam_re, lam_im,
           log_step, b_re, b_im, c_re, c_im, d_skip, w_glu, b_glu, w_ssm, w_out, final_g):
    bsz, seq, d = x.shape
    t = T_CHUNK
    e = w_conv.shape[0]
    g_n, p_n = lam_re.shape
    h_n = e // g_n
    width = dw_w.shape[0]
    nc = seq // t
    rows = bsz * nc
    assert meta.shape[0] == t and h_n * t == 2 * LANES_V7X and nc == LANES_V7X
    assert width <= 2 * t - 1 and seq % t == 0 and 2 * p_n == LANES_V7X and g_n % 2 == 0
    tn = _col_tile(e)
    ne = e // tn

    x4 = x.reshape(bsz, nc, t, d)
    w_in_b = w_in.astype(BF16)
    gain = norm_g.reshape(1, d)

    n_conv, n_gate, n_u, n_zs = ne, (2 * d) // tn, e // tn, e // tn
    c_gate, c_u, c_zs = n_conv, n_conv + n_gate, n_conv + n_gate + n_u
    gpb = tn // h_n

    def clip(v, lo, hi):
        return jnp.minimum(jnp.maximum(v, lo), hi)

    def wcol(off, start, count):
        return pl.BlockSpec((d, tn), lambda j, n: (0, off + clip(n - start, 0, count - 1)))

    a_slab, zc_slab, gates, ut, zst = pl.pallas_call(
        functools.partial(_proj_kernel, n_conv=n_conv, n_gate=n_gate, n_u=n_u),
        grid=(t, c_zs + n_zs),
        in_specs=[pl.BlockSpec(memory_space=pl.ANY),
                  pl.BlockSpec((1, d), lambda j, n: (0, 0)),
                  wcol(0, 0, n_conv), wcol(ne, 0, n_conv), wcol(2 * ne, 0, n_conv),
                  wcol(5 * ne, c_gate, n_gate),
                  pl.BlockSpec((1, tn), lambda j, n: (0, clip(n - c_gate, 0, n_gate - 1))),
                  wcol(3 * ne, c_u, n_u + n_zs)],
        out_specs=[pl.BlockSpec((None, rows, tn), lambda j, n: (j, 0, clip(n, 0, n_conv - 1))),
                   pl.BlockSpec((None, rows, tn), lambda j, n: (j, 0, clip(n, 0, n_conv - 1))),
                   pl.BlockSpec((None, rows, tn), lambda j, n: (j, 0, clip(n - c_gate, 0, n_gate - 1))),
                   pl.BlockSpec((gpb, h_n, rows), lambda j, n: (clip(n - c_u, 0, n_u - 1), j, 0)),
                   pl.BlockSpec((None, tn, rows), lambda j, n: (j, clip(n - c_zs, 0, n_zs - 1), 0))],
        out_shape=[jax.ShapeDtypeStruct((t, rows, e), F32),
                   jax.ShapeDtypeStruct((t, rows, e), BF16),
                   jax.ShapeDtypeStruct((t, rows, 2 * d), BF16),
                   jax.ShapeDtypeStruct((g_n, t * h_n, rows), F32),
                   jax.ShapeDtypeStruct((t, e, rows), BF16)],
        scratch_shapes=[pltpu.VMEM((bsz, nc, d), F32), pltpu.VMEM((rows, d), BF16),
                        pltpu.SemaphoreType.DMA((1,))],
        compiler_params=_cparams("arbitrary", "arbitrary"),
        name="proj",
    )(x4, gain, w_in_b, w_in_b, w_in_b, w_in_b, b_gate.reshape(1, 2 * d), w_in_b)

    a_meta, u_meta = pl.pallas_call(
        _proj_meta_kernel,
        grid=(ne,),
        in_specs=[pl.BlockSpec((t, d), lambda n: (0, 0)), pl.BlockSpec((1, d), lambda n: (0, 0)),
                  pl.BlockSpec((d, tn), lambda n: (0, n)), pl.BlockSpec((d, tn), lambda n: (0, ne + n)),
                  pl.BlockSpec((d, tn), lambda n: (0, 3 * ne + n))],
        out_specs=[pl.BlockSpec((t, tn), lambda n: (0, n)), pl.BlockSpec((t, tn), lambda n: (0, n))],
        out_shape=[jax.ShapeDtypeStruct((t, e), F32), jax.ShapeDtypeStruct((t, e), F32)],
        compiler_params=_cparams("arbitrary"),
        name="proj_meta",
    )(meta, gain, w_in_b, w_in_b, w_in_b)
    um_col = u_meta.reshape(t, g_n, h_n).transpose(1, 0, 2).reshape(g_n, t * h_n, 1)

    grp = lambda *shape: pl.BlockSpec((None,) + shape, lambda g: (g,) + (0,) * len(shape))
    tile2 = lambda v: jnp.concatenate([v, v], axis=-1)
    m_op, r_op, o_op, a_pow = pl.pallas_call(
        _s5_param_kernel,
        grid=(g_n,),
        in_specs=[grp(p_n, 1), grp(p_n, 1), grp(1, 2 * p_n), grp(1, 2 * p_n), grp(1, 1),
                  grp(p_n, t * h_n), grp(p_n, t * h_n), grp(h_n, 2 * p_n), grp(h_n, 2 * p_n)],
        out_specs=[grp(t * h_n, t * h_n), grp(2 * p_n, t * h_n), grp(t * h_n, 2 * p_n),
                   grp(2 * SUBLANES_V7X, 2 * p_n)],
        out_shape=[jax.ShapeDtypeStruct((g_n, t * h_n, t * h_n), F32),
                   jax.ShapeDtypeStruct((g_n, 2 * p_n, t * h_n), F32),
                   jax.ShapeDtypeStruct((g_n, t * h_n, 2 * p_n), F32),
                   jax.ShapeDtypeStruct((g_n, 2 * SUBLANES_V7X, 2 * p_n), F32)],
        compiler_params=_cparams("parallel"),
        name="s5_params",
    )(lam_re.reshape(g_n, p_n, 1), lam_im.reshape(g_n, p_n, 1),
      tile2(lam_re).reshape(g_n, 1, 2 * p_n), tile2(lam_im).reshape(g_n, 1, 2 * p_n),
      log_step.reshape(g_n, 1, 1),
      jnp.tile(b_re, (1, 1, t)), jnp.tile(b_im, (1, 1, t)),
      jnp.concatenate([c_re, c_im], axis=-1), jnp.concatenate([c_im, c_re], axis=-1))
    pw_pair = (a_pow[:, :, :p_n].reshape(g_n // 2, 2, 2 * SUBLANES_V7X, p_n)
               .transpose(0, 2, 1, 3).reshape(g_n // 2, 2 * SUBLANES_V7X, 2 * p_n))

    dcol = jnp.tile(d_skip.reshape(g_n, 1, h_n), (1, t, 1)).reshape(g_n, t * h_n, 1)
    pair = lambda *shape: pl.BlockSpec((2,) + shape, lambda g: (g,) + (0,) * len(shape))
    yt = pl.pallas_call(
        _s5_scan_kernel,
        grid=(g_n // 2,),
        in_specs=[pair(t * h_n, rows), pair(t * h_n, 1), pair(t * h_n, t * h_n), pair(2 * p_n, t * h_n),
                  pair(t * h_n, 2 * p_n), grp(2 * SUBLANES_V7X, 2 * p_n), pair(t * h_n, 1)],
        out_specs=pl.BlockSpec((t, 2 * h_n, rows), lambda g: (0, g, 0)),
        out_shape=jax.ShapeDtypeStruct((t, e, rows), F32),
        compiler_params=_cparams("parallel"),
        name="s5_scan",
    )(ut, um_col, m_op, r_op, o_op, pw_pair, dcol)

    rt = rows // 2
    y2t = pl.pallas_call(
        _glu_kernel,
        grid=(t, 2),
        in_specs=[pl.BlockSpec((None, e, rt), lambda j, r: (j, 0, r)),
                  pl.BlockSpec((None, e, rt), lambda j, r: (j, 0, r)),
                  pl.BlockSpec((e, e), lambda j, r: (0, 0)),
                  pl.BlockSpec((e, 1), lambda j, r: (0, 0))],
        out_specs=pl.BlockSpec((None, e, rt), lambda j, r: (j, 0, r)),
        out_shape=jax.ShapeDtypeStruct((t, e, rows), BF16),
        compiler_params=_cparams("parallel", "parallel"),
        name="s5_glu",
    )(yt, zst, w_glu.T.astype(BF16), b_glu.reshape(e, 1))

    cw = 256
    conv = pl.pallas_call(
        functools.partial(_conv_kernel, width=width),
        grid=(bsz, e // cw),
        in_specs=[pl.BlockSpec((t, nc, cw), lambda b, c: (0, b, c)),
                  pl.BlockSpec((t, cw), lambda b, c: (0, c)),
                  pl.BlockSpec((width, cw), lambda b, c: (0, c)),
                  pl.BlockSpec((1, cw), lambda b, c: (0, c))],
        out_specs=pl.BlockSpec((t, nc, cw), lambda b, c: (0, b, c)),
        out_shape=jax.ShapeDtypeStruct((t, rows, e), F32),
        scratch_shapes=[pltpu.VMEM((t, nc, cw), F32), pltpu.VMEM((t, nc, cw), F32)],
        compiler_params=_cparams("parallel", "parallel"),
        name="dw_conv",
    )(a_slab, a_meta, dw_w.reshape(width, e), dw_b.reshape(1, e))

    merged = pl.pallas_call(
        _merge_kernel,
        grid=(t, 2),
        in_specs=[pl.BlockSpec((None, rt, e), lambda j, r: (j, r, 0)),
                  pl.BlockSpec((None, rt, e), lambda j, r: (j, r, 0)),
                  pl.BlockSpec((None, rt, 2 * d), lambda j, r: (j, r, 0)),
                  pl.BlockSpec((None, e, rt), lambda j, r: (j, 0, r)),
                  pl.BlockSpec((1, e), lambda j, r: (0, 0)),
                  pl.BlockSpec((1, e), lambda j, r: (0, 0)),
                  pl.BlockSpec((e, d), lambda j, r: (0, 0)),
                  pl.BlockSpec((e, d), lambda j, r: (0, 0))],
        out_specs=pl.BlockSpec((None, rt, d), lambda j, r: (j, r, 0)),
        out_shape=jax.ShapeDtypeStruct((t, rows, d), BF16),
        compiler_params=_cparams("parallel", "parallel"),
        name="merge",
    )(conv, zc_slab, gates, y2t, ln_g.reshape(1, e), ln_b.reshape(1, e),
      w_conv.astype(BF16), w_ssm.astype(BF16))

    hb = bsz // 2
    out = pl.pallas_call(
        _out_kernel,
        grid=(t, 2),
        in_specs=[pl.BlockSpec((None, rt, d), lambda j, r: (j, r, 0)),
                  pl.BlockSpec((d, d), lambda j, r: (0, 0)),
                  pl.BlockSpec((1, d), lambda j, r: (0, 0)),
                  pl.BlockSpec(memory_space=pl.ANY)],
        out_specs=pl.BlockSpec(memory_space=pl.ANY),
        out_shape=jax.ShapeDtypeStruct((bsz, nc, t, d), F32),
        scratch_shapes=[pltpu.VMEM((2, hb, nc, d), F32), pltpu.VMEM((2, hb, nc, d), F32),
                        pltpu.SemaphoreType.DMA((2,)), pltpu.SemaphoreType.DMA((2,))],
        compiler_params=_cparams("arbitrary", "arbitrary"),
        name="out_proj",
    )(merged, w_out.astype(BF16), final_g.reshape(1, d), x4)
    return out.reshape(bsz, seq, d)
```

```python
import functools
import math

import jax
import jax.numpy as jnp
from jax import lax
from jax.experimental import pallas as pl
from jax.experimental.pallas import tpu as pltpu

F32 = jnp.float32
BF16 = jnp.bfloat16

T_CHUNK = 16
LANES_V7X = 128
SUBLANES_V7X = 8
NORM_EPS = 1e-6
LN_EPS = 1e-5
LAM_RE_MAX = -1e-4
VMEM_LIMIT_V7X = 56 * 1024 * 1024


def _cparams(*sem):
    return pltpu.CompilerParams(dimension_semantics=sem, vmem_limit_bytes=VMEM_LIMIT_V7X)


def _sigmoid(x):
    return 0.5 * jnp.tanh(0.5 * x) + 0.5


def _silu(x):
    return x * _sigmoid(x)


def _dot(a, b):
    return jnp.dot(a, b, preferred_element_type=F32)


def _dot_nt(a, b):
    return lax.dot_general(a, b, (((1,), (1,)), ((), ())), preferred_element_type=F32)


def _dot_tn(a, b):
    return lax.dot_general(a, b, (((0,), (0,)), ((), ())), preferred_element_type=F32)


def _dot_tt(a, b):
    return lax.dot_general(a, b, (((0,), (1,)), ((), ())), preferred_element_type=F32)


def _cmul(ar, ai, br, bi):
    return ar * br - ai * bi, ar * bi + ai * br


def _norm_slab(x_ref, gain_ref, xn_ref):
    nb, nc, _ = x_ref.shape
    gain = gain_ref[...]

    def body(b, carry):
        x = x_ref[b]
        ms = jnp.mean(x * x, axis=-1, keepdims=True)
        r0 = pl.multiple_of(b * nc, nc)
        xn_ref[pl.ds(r0, nc), :] = (x * lax.rsqrt(ms + NORM_EPS) * gain).astype(BF16)
        return carry

    lax.fori_loop(0, nb, body, 0)


PROJ_STEPS = 9
PROJ_SECTION = (1, 0, 2, 5, 6, 7, 8, 3, 4)
STEP_G, STEP_V, STEP_ZC, STEP_GATE0, STEP_U, STEP_ZS = 0, 1, 2, 3, 7, 8


def _proj_kernel(x_hbm, gain_ref, w_ref, bg_ref, a_ref, zc_ref, gt_ref, ut_ref, zt_ref,
                 xbuf, xn_ref, sem):
    j, n = pl.program_id(0), pl.program_id(1)

    def slab_copy(jj):
        return pltpu.make_async_copy(x_hbm.at[:, :, jj, :], xbuf, sem.at[0])

    @pl.when(n == 0)
    def _():
        @pl.when(j == 0)
        def _():
            slab_copy(0).start()

        slab_copy(j).wait()
        _norm_slab(xbuf, gain_ref, xn_ref)

        @pl.when(j + 1 < pl.num_programs(0))
        def _():
            slab_copy(j + 1).start()

    @pl.when(n == STEP_G)
    def _():
        a_ref[...] = _sigmoid(_dot(xn_ref[...], w_ref[...]))

    @pl.when(n == STEP_V)
    def _():
        a_ref[...] = a_ref[...] * _dot(xn_ref[...], w_ref[...])

    @pl.when(n == STEP_ZC)
    def _():
        zc_ref[...] = _silu(_dot(xn_ref[...], w_ref[...])).astype(BF16)

    @pl.when((n >= STEP_GATE0) & (n < STEP_U))
    def _():
        gt_ref[...] = _sigmoid(_dot(xn_ref[...], w_ref[...]) + bg_ref[...]).astype(BF16)

    @pl.when(n == STEP_U)
    def _():
        acc = _dot_tt(w_ref[...], xn_ref[...])
        ut_ref[...] = acc.reshape(ut_ref.shape)

    @pl.when(n == STEP_ZS)
    def _():
        zt_ref[...] = _silu(_dot_tt(w_ref[...], xn_ref[...])).astype(BF16)


def _proj_meta_kernel(m_ref, gain_ref, wv_ref, wg_ref, wu_ref, a_ref, u_ref):
    x = m_ref[...]
    ms = jnp.mean(x * x, axis=-1, keepdims=True)
    xn = (x * lax.rsqrt(ms + NORM_EPS) * gain_ref[...]).astype(BF16)
    a_ref[...] = _dot(xn, wv_ref[...]) * _sigmoid(_dot(xn, wg_ref[...]))
    u_ref[...] = _dot(xn, wu_ref[...])


def _lam_bar(lr, li, dt):
    lr = jnp.minimum(lr, LAM_RE_MAX)
    mag = jnp.exp(lr * dt)
    th = li * dt
    return lr, mag * jnp.cos(th), mag * jnp.sin(th)


def _s5_param_kernel(*refs):
    for g in range(refs[0].shape[0]):
        _s5_param_group(*[ref.at[g] for ref in refs])


def _s5_param_group(lrc_ref, lic_ref, lrr_ref, lir_ref, ls_ref, btr_ref, bti_ref,
                    ca_ref, cb_ref, m_ref, r_ref, o_ref, pw_ref):
    p = lrc_ref.shape[0]
    t = T_CHUNK
    h = btr_ref.shape[1] // t
    dt = jnp.exp(ls_ref[...])

    lr, lam_r, lam_i = _lam_bar(lrc_ref[...], lic_ref[...], dt)
    li = lic_ref[...]
    den = lr * lr + li * li
    nr, ni = lam_r - 1.0, lam_i
    kr = (nr * lr + ni * li) / den
    ki = (ni * lr - nr * li) / den
    bbr, bbi = _cmul(kr, ki, btr_ref[...], bti_ref[...])

    lane = lax.broadcasted_iota(jnp.int32, (p, t * h), 1)
    expo = (t - 1) - lane // h
    pr = jnp.ones((p, t * h), F32)
    pi = jnp.zeros((p, t * h), F32)
    br_, bi_ = lam_r, lam_i
    for k in range(int(math.log2(t))):
        qr, qi = _cmul(pr, pi, br_, bi_)
        bit = ((expo >> k) & 1) == 1
        pr = jnp.where(bit, qr, pr)
        pi = jnp.where(bit, qi, pi)
        br_, bi_ = _cmul(br_, bi_, br_, bi_)
    wr, wi = _cmul(pr, pi, bbr, bbi)
    r_ref[0:p, :] = wr
    r_ref[p:2 * p, :] = wi

    ca = ca_ref[...]
    cb = cb_ref[...]
    sgn = jnp.where(lax.broadcasted_iota(jnp.int32, ca.shape, 1) < p, 1.0, -1.0)
    w = jnp.concatenate([wr, wi], axis=0)
    krev = jnp.dot(sgn * ca, w, preferred_element_type=F32, precision=lax.Precision.HIGHEST)

    half = LANES_V7X
    ka, kb = krev[:, :half], krev[:, half:]
    lane_h = lax.broadcasted_iota(jnp.int32, (h, half), 1)
    zero = jnp.zeros((h, half), F32)
    for i in range(t):
        s = (t - 1 - i) * h
        if s == 0:
            lo, hi = ka, kb
        elif s < half:
            ra = pltpu.roll(ka, half - s, 1)
            rb = pltpu.roll(kb, half - s, 1)
            keep = lane_h < (half - s)
            lo, hi = jnp.where(keep, ra, rb), jnp.where(keep, rb, zero)
        elif s == half:
            lo, hi = kb, zero
        else:
            rb = pltpu.roll(kb, 2 * half - s, 1)
            lo, hi = jnp.where(lane_h < (2 * half - s), rb, zero), zero
        m_ref[i * h:(i + 1) * h, 0:half] = lo
        m_ref[i * h:(i + 1) * h, half:2 * half] = hi

    _, l1r, l1i = _lam_bar(lrr_ref[...], lir_ref[...], dt)
    qr, qi = l1r, l1i
    for i in range(t):
        o_ref[i * h:(i + 1) * h, :] = sgn * (ca * qr) - cb * qi
        qr, qi = _cmul(qr, qi, l1r, l1i)

    ar, ai = l1r, l1i
    for _ in range(int(math.log2(t))):
        ar, ai = _cmul(ar, ai, ar, ai)
    qr, qi = ar, ai
    for s in range(SUBLANES_V7X):
        pw_ref[s:s + 1, :] = qr
        pw_ref[SUBLANES_V7X + s:SUBLANES_V7X + s + 1, :] = qi
        qr, qi = _cmul(qr, qi, ar, ai)


def _s5_scan_kernel(ut_ref, um_ref, m_ref, r_ref, o_ref, pw_ref, dcol_ref, yt_ref):
    ng, th, rows = ut_ref.shape
    p = r_ref.shape[1] // 2
    lw, sl = LANES_V7X, SUBLANES_V7X
    nb = rows // lw
    h = yt_ref.shape[1] // ng
    lane0 = lax.broadcasted_iota(jnp.int32, (th, lw), 1) == 0

    ys, zre, zim = [], [], []
    for g in range(ng):
        meta_blk = jnp.where(lane0, um_ref[g], 0.0)
        ue = jnp.concatenate([ut_ref[g], meta_blk], axis=1).astype(BF16)
        ys.append(_dot(m_ref[g].astype(BF16), ue[:, :rows]))
        z = _dot(r_ref[g].astype(BF16), ue)
        zre.append(z[:p])
        zim.append(z[p:])
    zr = jnp.concatenate(zre, axis=0)
    zi = jnp.concatenate(zim, axis=0)

    def chunk_major(z):
        return jnp.concatenate([z[:, b * lw:(b + 1) * lw].T for b in range(nb)], axis=1)

    def tile_lanes(v):
        return jnp.concatenate([v] * nb, axis=1)

    er, ei = chunk_major(zr), chunk_major(zi)
    car_r = tile_lanes(zr[:, rows:rows + lw].T[0:1, :])
    car_i = tile_lanes(zi[:, rows:rows + lw].T[0:1, :])
    pw = tile_lanes(pw_ref[...])
    nc, width = er.shape
    srow = lax.broadcasted_iota(jnp.int32, (nc, width), 0) % sl

    def tile_roll(x, sh):
        return pltpu.roll(x.reshape(nc // sl, sl, width), sh, 1).reshape(nc, width)

    for sh in (1, 2, 4):
        dr, di = _cmul(pw[sh - 1:sh], pw[sl + sh - 1:sl + sh], tile_roll(er, sh), tile_roll(ei, sh))
        ok = srow >= sh
        er = er + jnp.where(ok, dr, 0.0)
        ei = ei + jnp.where(ok, di, 0.0)

    a8r, a8i = pw[0:sl], pw[sl:2 * sl]
    first = lax.broadcasted_iota(jnp.int32, (sl, width), 0) == 0
    xr_rows, xi_rows = [], []
    for r in range(nc // sl):
        cr, ci = _cmul(a8r, a8i, car_r, car_i)
        fr = er[r * sl:(r + 1) * sl] + cr
        fi = ei[r * sl:(r + 1) * sl] + ci
        xr_rows.append(jnp.where(first, car_r, pltpu.roll(fr, 1, 0)))
        xi_rows.append(jnp.where(first, car_i, pltpu.roll(fi, 1, 0)))
        car_r, car_i = fr[sl - 1:sl], fi[sl - 1:sl]
    sr = jnp.concatenate(xr_rows, axis=0)
    si = jnp.concatenate(xi_rows, axis=0)

    def state_major(s):
        return jnp.concatenate([s[:, b * lw:(b + 1) * lw].T for b in range(nb)], axis=1)

    srt, sit = state_major(sr), state_major(si)
    for g in range(ng):
        s_in = jnp.concatenate([srt[g * p:(g + 1) * p], sit[g * p:(g + 1) * p]], axis=0).astype(BF16)
        y = ys[g] + _dot(o_ref[g].astype(BF16), s_in) + dcol_ref[g] * ut_ref[g]
        yt_ref[:, g * h:(g + 1) * h, :] = y.reshape(th // h, h, rows)


def _conv_branch_kernel(a_ref, halo_ref, am_ref, zc_ref, gc_ref, w_ref, b_ref, lng_ref, lnb_ref,
                        wc_ref, o_ref, s1_ref, s2_ref, yc_ref, cv_ref, *, width, ranges_per_seq):
    t, nr, e = a_ref.shape
    hr = halo_ref.shape[1]
    d = o_ref.shape[-1]
    seq_start = (pl.program_id(0) % ranges_per_seq) == 0
    for j in range(t):
        p1 = jnp.where(seq_start, am_ref[j:j + 1, :], halo_ref[j, hr - 1:hr, :])
        p2 = jnp.where(seq_start, 0.0, halo_ref[j, hr - 2:hr - 1, :])
        s1_ref[j, 0:1, :] = p1
        s1_ref[j, 1:nr, :] = a_ref[j, 0:nr - 1, :]
        s2_ref[j, 0:1, :] = p2
        s2_ref[j, 1:2, :] = p1
        s2_ref[j, 2:nr, :] = a_ref[j, 0:nr - 2, :]
    srcs = (a_ref, s1_ref, s2_ref)
    lng, lnb = lng_ref[...], lnb_ref[...]
    lw = LANES_V7X
    sl = w_ref.shape[1]
    for c0 in range(0, e, lw):
        wk = [w_ref[k, :, c0:c0 + lw] for k in range(width)]
        bias = jnp.broadcast_to(b_ref[:, c0:c0 + lw], (sl, lw))

        def row_tile(i, carry, c0=c0, wk=wk, bias=bias):
            r0 = pl.multiple_of(i * sl, sl)
            for jo in range(t):
                acc = bias
                for s in range(width):
                    r, q = s % t, s // t
                    src = srcs[q + (1 if r > jo else 0)]
                    acc = acc + wk[width - 1 - s] * src[(jo - r) % t, pl.ds(r0, sl), c0:c0 + lw]
                cv_ref[jo, pl.ds(r0, sl), c0:c0 + lw] = acc
            return carry

        lax.fori_loop(0, nr // sl, row_tile, 0)
    for jo in range(t):
        acc = cv_ref[jo]
        mu = jnp.mean(acc, axis=-1, keepdims=True)
        xc = acc - mu
        var = jnp.mean(xc * xc, axis=-1, keepdims=True)
        yn = xc * lax.rsqrt(var + LN_EPS) * lng + lnb
        yc_ref[jo * nr:(jo + 1) * nr, :] = (_silu(yn) * zc_ref[jo].astype(F32)).astype(BF16)
    pc = _dot(yc_ref[...], wc_ref[...])
    gate = gc_ref[...].reshape(t * nr, d).astype(F32)
    o_ref[...] = (gate * pc).astype(BF16).reshape(t, nr, d)


def _ssm_out_kernel(yt_ref, zt_ref, mc_ref, gs_ref, wg_ref, bcol_ref, ws_ref, w_ref, gain_ref,
                    x_hbm, o_hbm, xbuf, obuf, sem_in, sem_out):
    j, r = pl.program_id(0), pl.program_id(1)
    nr = pl.num_programs(1)
    step, n_steps = j * nr + r, pl.num_programs(0) * nr
    _, hb, nc, d = xbuf.shape
    slot = step % 2

    def x_copy(jj, rr, sl):
        return pltpu.make_async_copy(x_hbm.at[pl.ds(rr * hb, hb), :, jj, :], xbuf.at[sl], sem_in.at[sl])

    def o_copy(jj, rr, sl):
        return pltpu.make_async_copy(obuf.at[sl], o_hbm.at[pl.ds(rr * hb, hb), :, jj, :], sem_out.at[sl])

    @pl.when(step == 0)
    def _():
        x_copy(0, 0, 0).start()

    x_copy(j, r, slot).wait()

    @pl.when(step + 1 < n_steps)
    def _():
        nxt = step + 1
        x_copy(nxt // nr, nxt % nr, 1 - slot).start()

    @pl.when(step >= 2)
    def _():
        o_copy(j, r, slot).wait()

    y = jax.nn.gelu(yt_ref[...], approximate=True)
    g = _dot(wg_ref[...], y.astype(BF16)) + bcol_ref[...]
    y2t = (y * _sigmoid(g) * zt_ref[...].astype(F32)).astype(BF16)
    ps = _dot_tn(y2t, ws_ref[...])
    merged = (mc_ref[...].astype(F32) + gs_ref[...].astype(F32) * ps).astype(BF16)
    hres = xbuf[slot].reshape(hb * nc, d) + _dot(merged, w_ref[...])
    ms = jnp.mean(hres * hres, axis=-1, keepdims=True)
    obuf[slot] = (hres * lax.rsqrt(ms + NORM_EPS) * gain_ref[...]).reshape(hb, nc, d)
    o_copy(j, r, slot).start()

    @pl.when(step == n_steps - 1)
    def _():
        o_copy(j, r, 1 - slot).wait()
        o_copy(j, r, slot).wait()


def _col_tile(n):
    return 512 if n % 512 == 0 else n


def kernel(x, meta, norm_g, w_in, b_gate, dw_w, dw_b, ln_g, ln_b, w_conv, lam_re, lam_im,
           log_step, b_re, b_im, c_re, c_im, d_skip, w_glu, b_glu, w_ssm, w_out, final_g):
    bsz, seq, d = x.shape
    t = T_CHUNK
    e = w_conv.shape[0]
    g_n, p_n = lam_re.shape
    h_n = e // g_n
    width = dw_w.shape[0]
    nc = seq // t
    rows = bsz * nc
    assert meta.shape[0] == t and h_n * t == 2 * LANES_V7X and nc == LANES_V7X
    assert width <= 2 * t - 1 and seq % t == 0 and 2 * p_n == LANES_V7X and g_n % 2 == 0
    tn = _col_tile(e)
    ne = e // tn

    x4 = x.reshape(bsz, nc, t, d)
    w_in_b = w_in.astype(BF16)
    gain = norm_g.reshape(1, d)

    assert 2 * d == (STEP_U - STEP_GATE0) * e and w_in.shape[1] == PROJ_STEPS * e

    def section(n):
        sec = jnp.int32(PROJ_SECTION[-1])
        for step in range(PROJ_STEPS - 2, -1, -1):
            sec = jnp.where(n == step, PROJ_SECTION[step], sec)
        return sec

    def gate_blk(n):
        return jnp.minimum(jnp.maximum(n - STEP_GATE0, 0), STEP_U - STEP_GATE0 - 1)

    a_slab, zc_slab, gates, ut, zst = pl.pallas_call(
        _proj_kernel,
        grid=(t, PROJ_STEPS),
        in_specs=[pl.BlockSpec(memory_space=pl.ANY),
                  pl.BlockSpec((1, d), lambda j, n: (0, 0)),
                  pl.BlockSpec((d, e), lambda j, n: (0, section(n))),
                  pl.BlockSpec((1, e), lambda j, n: (0, gate_blk(n)))],
        out_specs=[pl.BlockSpec((None, rows, e), lambda j, n: (j, 0, 0)),
                   pl.BlockSpec((None, rows, e), lambda j, n: (j, 0, 0)),
                   pl.BlockSpec((None, rows, e), lambda j, n: (j, 0, gate_blk(n))),
                   pl.BlockSpec((g_n, h_n, rows), lambda j, n: (0, j, 0)),
                   pl.BlockSpec((None, e, rows), lambda j, n: (j, 0, 0))],
        out_shape=[jax.ShapeDtypeStruct((t, rows, e), F32),
                   jax.ShapeDtypeStruct((t, rows, e), BF16),
                   jax.ShapeDtypeStruct((t, rows, 2 * d), BF16),
                   jax.ShapeDtypeStruct((g_n, t * h_n, rows), F32),
                   jax.ShapeDtypeStruct((t, e, rows), BF16)],
        scratch_shapes=[pltpu.VMEM((bsz, nc, d), F32), pltpu.VMEM((rows, d), BF16),
                        pltpu.SemaphoreType.DMA((1,))],
        compiler_params=_cparams("arbitrary", "arbitrary"),
        name="proj",
    )(x4, gain, w_in_b, b_gate.reshape(1, 2 * d))

    a_meta, u_meta = pl.pallas_call(
        _proj_meta_kernel,
        grid=(ne,),
        in_specs=[pl.BlockSpec((t, d), lambda n: (0, 0)), pl.BlockSpec((1, d), lambda n: (0, 0)),
                  pl.BlockSpec((d, tn), lambda n: (0, n)), pl.BlockSpec((d, tn), lambda n: (0, ne + n)),
                  pl.BlockSpec((d, tn), lambda n: (0, 3 * ne + n))],
        out_specs=[pl.BlockSpec((t, tn), lambda n: (0, n)), pl.BlockSpec((t, tn), lambda n: (0, n))],
        out_shape=[jax.ShapeDtypeStruct((t, e), F32), jax.ShapeDtypeStruct((t, e), F32)],
        compiler_params=_cparams("arbitrary"),
        name="proj_meta",
    )(meta, gain, w_in_b, w_in_b, w_in_b)
    um_col = u_meta.reshape(t, g_n, h_n).transpose(1, 0, 2).reshape(g_n, t * h_n, 1)

    grp = lambda *shape: pl.BlockSpec((None,) + shape, lambda g: (g,) + (0,) * len(shape))
    tile2 = lambda v: jnp.concatenate([v, v], axis=-1)
    gpp = 8
    gblk = lambda *shape: pl.BlockSpec((gpp,) + shape, lambda g: (g,) + (0,) * len(shape))
    m_op, r_op, o_op, a_pow = pl.pallas_call(
        _s5_param_kernel,
        grid=(g_n // gpp,),
        in_specs=[gblk(p_n, 1), gblk(p_n, 1), gblk(1, 2 * p_n), gblk(1, 2 * p_n), gblk(1, 1),
                  gblk(p_n, t * h_n), gblk(p_n, t * h_n), gblk(h_n, 2 * p_n), gblk(h_n, 2 * p_n)],
        out_specs=[gblk(t * h_n, t * h_n), gblk(2 * p_n, t * h_n), gblk(t * h_n, 2 * p_n),
                   gblk(2 * SUBLANES_V7X, 2 * p_n)],
        out_shape=[jax.ShapeDtypeStruct((g_n, t * h_n, t * h_n), F32),
                   jax.ShapeDtypeStruct((g_n, 2 * p_n, t * h_n), F32),
                   jax.ShapeDtypeStruct((g_n, t * h_n, 2 * p_n), F32),
                   jax.ShapeDtypeStruct((g_n, 2 * SUBLANES_V7X, 2 * p_n), F32)],
        compiler_params=_cparams("parallel"),
        name="s5_params",
    )(lam_re.reshape(g_n, p_n, 1), lam_im.reshape(g_n, p_n, 1),
      tile2(lam_re).reshape(g_n, 1, 2 * p_n), tile2(lam_im).reshape(g_n, 1, 2 * p_n),
      log_step.reshape(g_n, 1, 1),
      jnp.tile(b_re, (1, 1, t)), jnp.tile(b_im, (1, 1, t)),
      jnp.concatenate([c_re, c_im], axis=-1), jnp.concatenate([c_im, c_re], axis=-1))
    pw_pair = (a_pow[:, :, :p_n].reshape(g_n // 2, 2, 2 * SUBLANES_V7X, p_n)
               .transpose(0, 2, 1, 3).reshape(g_n // 2, 2 * SUBLANES_V7X, 2 * p_n))

    dcol = jnp.tile(d_skip.reshape(g_n, 1, h_n), (1, t, 1)).reshape(g_n, t * h_n, 1)
    pair = lambda *shape: pl.BlockSpec((2,) + shape, lambda g: (g,) + (0,) * len(shape))
    yt = pl.pallas_call(
        _s5_scan_kernel,
        grid=(g_n // 2,),
        in_specs=[pair(t * h_n, rows), pair(t * h_n, 1), pair(t * h_n, t * h_n), pair(2 * p_n, t * h_n),
                  pair(t * h_n, 2 * p_n), grp(2 * SUBLANES_V7X, 2 * p_n), pair(t * h_n, 1)],
        out_specs=pl.BlockSpec((t, 2 * h_n, rows), lambda g: (0, g, 0)),
        out_shape=jax.ShapeDtypeStruct((t, e, rows), F32),
        compiler_params=_cparams("parallel"),
        name="s5_scan",
    )(ut, um_col, m_op, r_op, o_op, pw_pair, dcol)

    nr = 32
    hr = SUBLANES_V7X
    ranges_per_seq = nc // nr
    mc = pl.pallas_call(
        functools.partial(_conv_branch_kernel, width=width, ranges_per_seq=ranges_per_seq),
        grid=(rows // nr,),
        in_specs=[pl.BlockSpec((t, nr, e), lambda q: (0, q, 0)),
                  pl.BlockSpec((t, hr, e), lambda q: (0, jnp.maximum(q * (nr // hr) - 1, 0), 0)),
                  pl.BlockSpec((t, e), lambda q: (0, 0)),
                  pl.BlockSpec((t, nr, e), lambda q: (0, q, 0)),
                  pl.BlockSpec((t, nr, d), lambda q: (0, q, 0)),
                  pl.BlockSpec((width, SUBLANES_V7X, e), lambda q: (0, 0, 0)),
                  pl.BlockSpec((1, e), lambda q: (0, 0)),
                  pl.BlockSpec((1, e), lambda q: (0, 0)),
                  pl.BlockSpec((1, e), lambda q: (0, 0)),
                  pl.BlockSpec((e, d), lambda q: (0, 0))],
        out_specs=pl.BlockSpec((t, nr, d), lambda q: (0, q, 0)),
        out_shape=jax.ShapeDtypeStruct((t, rows, d), BF16),
        scratch_shapes=[pltpu.VMEM((t, nr, e), F32), pltpu.VMEM((t, nr, e), F32),
                        pltpu.VMEM((t * nr, e), BF16), pltpu.VMEM((t, nr, e), F32)],
        compiler_params=_cparams("parallel"),
        name="conv_branch",
    )(a_slab, a_slab, a_meta, zc_slab, gates,
      jnp.broadcast_to(dw_w.reshape(width, 1, e), (width, SUBLANES_V7X, e)), dw_b.reshape(1, e),
      ln_g.reshape(1, e), ln_b.reshape(1, e), w_conv.astype(BF16))

    rt = rows // 2
    hb = bsz // 2
    const = lambda *shape: pl.BlockSpec(shape, lambda j, r: (0,) * len(shape), pipeline_mode=pl.Buffered(1))
    out = pl.pallas_call(
        _ssm_out_kernel,
        grid=(t, 2),
        in_specs=[pl.BlockSpec((None, e, rt), lambda j, r: (j, 0, r)),
                  pl.BlockSpec((None, e, rt), lambda j, r: (j, 0, r)),
                  pl.BlockSpec((None, rt, d), lambda j, r: (j, r, 0)),
                  pl.BlockSpec((None, rt, d), lambda j, r: (j, r, 1)),
                  const(e, e), const(e, 1), const(e, d), const(d, d), const(1, d),
                  pl.BlockSpec(memory_space=pl.ANY)],
        out_specs=pl.BlockSpec(memory_space=pl.ANY),
        out_shape=jax.ShapeDtypeStruct((bsz, nc, t, d), F32),
        scratch_shapes=[pltpu.VMEM((2, hb, nc, d), F32), pltpu.VMEM((2, hb, nc, d), F32),
                        pltpu.SemaphoreType.DMA((2,)), pltpu.SemaphoreType.DMA((2,))],
        compiler_params=_cparams("arbitrary", "arbitrary"),
        name="ssm_out",
    )(yt, zst, mc, gates, w_glu.T.astype(BF16), b_glu.reshape(e, 1), w_ssm.astype(BF16),
      w_out.astype(BF16), final_g.reshape(1, d), x4)
    return out.reshape(bsz, seq, d)
```

```python
import functools
import math

import jax
import jax.numpy as jnp
from jax import lax
from jax.experimental import pallas as pl
from jax.experimental.pallas import tpu as pltpu

F32 = jnp.float32
BF16 = jnp.bfloat16

T_CHUNK = 16
LANES_V7X = 128
SUBLANES_V7X = 8
NORM_EPS = 1e-6
LN_EPS = 1e-5
LAM_RE_MAX = -1e-4
VMEM_LIMIT_V7X = 56 * 1024 * 1024


def _cparams(*sem):
    return pltpu.CompilerParams(dimension_semantics=sem, vmem_limit_bytes=VMEM_LIMIT_V7X)


def _sigmoid(x):
    return 0.5 * jnp.tanh(0.5 * x) + 0.5


def _silu(x):
    return x * _sigmoid(x)


def _dot(a, b):
    return jnp.dot(a, b, preferred_element_type=F32)


def _dot_nt(a, b):
    return lax.dot_general(a, b, (((1,), (1,)), ((), ())), preferred_element_type=F32)


def _dot_tn(a, b):
    return lax.dot_general(a, b, (((0,), (0,)), ((), ())), preferred_element_type=F32)


def _dot_tt(a, b):
    return lax.dot_general(a, b, (((0,), (1,)), ((), ())), preferred_element_type=F32)


def _cmul(ar, ai, br, bi):
    return ar * br - ai * bi, ar * bi + ai * br


def _norm_slab(x_ref, gain_ref, xn_ref):
    nb, nc, _ = x_ref.shape
    gain = gain_ref[...]

    def body(b, carry):
        x = x_ref[b]
        ms = jnp.mean(x * x, axis=-1, keepdims=True)
        r0 = pl.multiple_of(b * nc, nc)
        xn_ref[pl.ds(r0, nc), :] = (x * lax.rsqrt(ms + NORM_EPS) * gain).astype(BF16)
        return carry

    lax.fori_loop(0, nb, body, 0)


PROJ_STEPS = 9
PROJ_SECTION = (1, 0, 2, 5, 6, 7, 8, 3, 4)
STEP_G, STEP_V, STEP_ZC, STEP_GATE0, STEP_U, STEP_ZS = 0, 1, 2, 3, 7, 8
STEP_NORM_NEXT = 4
N_GATE_BLOCKS = STEP_U - STEP_GATE0


def _proj_kernel(x_hbm, w_hbm, gain_ref, bg_ref, a_hbm, zc_hbm, gt_hbm, ut_hbm, zt_hbm,
                 xbuf, xn_ref, wbuf, a_st, zc_st, gt_st, ut_st, zt_st, sem_x, sem_w, sem_o):
    j, nj = pl.program_id(0), pl.num_programs(0)
    e = wbuf.shape[2]
    h = ut_st.shape[1]
    cur = j % 2
    not_first = j > 0

    def x_copy(jj):
        return pltpu.make_async_copy(x_hbm.at[:, :, jj, :], xbuf, sem_x.at[0])

    def w_copy(k, slot):
        return pltpu.make_async_copy(w_hbm.at[:, pl.ds(PROJ_SECTION[k] * e, e)], wbuf.at[slot], sem_w.at[slot])

    def a_copy():
        return pltpu.make_async_copy(a_st, a_hbm.at[j], sem_o.at[0])

    def zc_copy():
        return pltpu.make_async_copy(zc_st, zc_hbm.at[j], sem_o.at[1])

    def gt_copy(g):
        return pltpu.make_async_copy(gt_st.at[g % 2], gt_hbm.at[j, :, pl.ds(g * e, e)], sem_o.at[2 + g % 2])

    def ut_copy():
        return pltpu.make_async_copy(ut_st, ut_hbm.at[:, pl.ds(j * h, h), :], sem_o.at[4])

    def zt_copy():
        return pltpu.make_async_copy(zt_st, zt_hbm.at[j], sem_o.at[5])

    def wait_previous(copy):
        @pl.when(not_first)
        def _():
            copy.wait()

    @pl.when(j == 0)
    def _():
        x_copy(0).start()
        w_copy(0, 0).start()
        x_copy(0).wait()
        _norm_slab(xbuf, gain_ref, xn_ref.at[0])

        @pl.when(nj > 1)
        def _():
            x_copy(1).start()

    xn = xn_ref.at[cur]

    def free_staging(k):
        if k == STEP_G:
            wait_previous(a_copy())
        elif k == STEP_ZC:
            wait_previous(zc_copy())
        elif STEP_GATE0 <= k < STEP_U:
            g = k - STEP_GATE0
            if g < 2:
                wait_previous(gt_copy(g))
            else:
                gt_copy(g).wait()
        elif k == STEP_U:
            wait_previous(ut_copy())
        elif k == STEP_ZS:
            wait_previous(zt_copy())

    def finish(k, acc):
        if k == STEP_G:
            a_st[...] = _sigmoid(acc)
        elif k == STEP_V:
            a_st[...] = a_st[...] * acc
            a_copy().start()
        elif k == STEP_ZC:
            zc_st[...] = _silu(acc).astype(BF16)
            zc_copy().start()
        elif k < STEP_U:
            g = k - STEP_GATE0
            gt_st[g % 2] = _sigmoid(acc + bg_ref[:, g * e:(g + 1) * e]).astype(BF16)
            gt_copy(g).start()
        elif k == STEP_U:
            ut_st[...] = acc.reshape(ut_st.shape).astype(BF16)
            ut_copy().start()
        else:
            zt_st[...] = _silu(acc).astype(BF16)
            zt_copy().start()

    for k in range(PROJ_STEPS):
        slot = (cur + k) % 2
        w_copy(k, slot).wait()
        if k + 1 < PROJ_STEPS:
            w_copy(k + 1, 1 - slot).start()
        else:
            @pl.when(j + 1 < nj)
            def _():
                w_copy(0, 1 - slot).start()
        free_staging(k)
        w = wbuf[slot]
        finish(k, _dot_tt(w, xn[...]) if k >= STEP_U else _dot(xn[...], w))
        if k == STEP_NORM_NEXT:
            @pl.when(j + 1 < nj)
            def _():
                x_copy(j + 1).wait()
                _norm_slab(xbuf, gain_ref, xn_ref.at[1 - cur])

                @pl.when(j + 2 < nj)
                def _():
                    x_copy(j + 2).start()

    @pl.when(j == nj - 1)
    def _():
        a_copy().wait()
        zc_copy().wait()
        gt_copy(N_GATE_BLOCKS - 2).wait()
        gt_copy(N_GATE_BLOCKS - 1).wait()
        ut_copy().wait()
        zt_copy().wait()


def _proj_meta_kernel(m_ref, gain_ref, wv_ref, wg_ref, wu_ref, a_ref, u_ref):
    x = m_ref[...]
    ms = jnp.mean(x * x, axis=-1, keepdims=True)
    xn = (x * lax.rsqrt(ms + NORM_EPS) * gain_ref[...]).astype(BF16)
    a_ref[...] = _dot(xn, wv_ref[...]) * _sigmoid(_dot(xn, wg_ref[...]))
    u_ref[...] = _dot(xn, wu_ref[...])


def _lam_bar(lr, li, dt):
    lr = jnp.minimum(lr, LAM_RE_MAX)
    mag = jnp.exp(lr * dt)
    th = li * dt
    return lr, mag * jnp.cos(th), mag * jnp.sin(th)


def _s5_param_kernel(*refs):
    for g in range(refs[0].shape[0]):
        _s5_param_group(*[ref.at[g] for ref in refs])


def _s5_param_group(lrc_ref, lic_ref, lrr_ref, lir_ref, ls_ref, btr_ref, bti_ref,
                    ca_ref, cb_ref, m_ref, r_ref, o_ref, pw_ref):
    p = lrc_ref.shape[0]
    t = T_CHUNK
    h = btr_ref.shape[1] // t
    dt = jnp.exp(ls_ref[...])

    lr, lam_r, lam_i = _lam_bar(lrc_ref[...], lic_ref[...], dt)
    li = lic_ref[...]
    den = lr * lr + li * li
    nr, ni = lam_r - 1.0, lam_i
    kr = (nr * lr + ni * li) / den
    ki = (ni * lr - nr * li) / den
    bbr, bbi = _cmul(kr, ki, btr_ref[...], bti_ref[...])

    lane = lax.broadcasted_iota(jnp.int32, (p, t * h), 1)
    expo = (t - 1) - lane // h
    pr = jnp.ones((p, t * h), F32)
    pi = jnp.zeros((p, t * h), F32)
    br_, bi_ = lam_r, lam_i
    for k in range(int(math.log2(t))):
        qr, qi = _cmul(pr, pi, br_, bi_)
        bit = ((expo >> k) & 1) == 1
        pr = jnp.where(bit, qr, pr)
        pi = jnp.where(bit, qi, pi)
        br_, bi_ = _cmul(br_, bi_, br_, bi_)
    wr, wi = _cmul(pr, pi, bbr, bbi)
    r_ref[0:p, :] = wr.astype(BF16)
    r_ref[p:2 * p, :] = wi.astype(BF16)

    ca = ca_ref[...]
    cb = cb_ref[...]
    sgn = jnp.where(lax.broadcasted_iota(jnp.int32, ca.shape, 1) < p, 1.0, -1.0)
    w = jnp.concatenate([wr, wi], axis=0)
    krev = jnp.dot(sgn * ca, w, preferred_element_type=F32, precision=lax.Precision.HIGHEST)

    half = LANES_V7X
    ka, kb = krev[:, :half], krev[:, half:]
    lane_h = lax.broadcasted_iota(jnp.int32, (h, half), 1)
    zero = jnp.zeros((h, half), F32)
    for i in range(t):
        s = (t - 1 - i) * h
        if s == 0:
            lo, hi = ka, kb
        elif s < half:
            ra = pltpu.roll(ka, half - s, 1)
            rb = pltpu.roll(kb, half - s, 1)
            keep = lane_h < (half - s)
            lo, hi = jnp.where(keep, ra, rb), jnp.where(keep, rb, zero)
        elif s == half:
            lo, hi = kb, zero
        else:
            rb = pltpu.roll(kb, 2 * half - s, 1)
            lo, hi = jnp.where(lane_h < (2 * half - s), rb, zero), zero
        m_ref[i * h:(i + 1) * h, 0:half] = lo.astype(BF16)
        m_ref[i * h:(i + 1) * h, half:2 * half] = hi.astype(BF16)

    _, l1r, l1i = _lam_bar(lrr_ref[...], lir_ref[...], dt)
    qr, qi = l1r, l1i
    for i in range(t):
        o_ref[i * h:(i + 1) * h, :] = (sgn * (ca * qr) - cb * qi).astype(BF16)
        qr, qi = _cmul(qr, qi, l1r, l1i)

    ar, ai = l1r, l1i
    for _ in range(int(math.log2(t))):
        ar, ai = _cmul(ar, ai, ar, ai)
    qr, qi = ar, ai
    for s in range(SUBLANES_V7X):
        pw_ref[s:s + 1, :] = qr
        pw_ref[SUBLANES_V7X + s:SUBLANES_V7X + s + 1, :] = qi
        qr, qi = _cmul(qr, qi, ar, ai)


def _s5_scan_kernel(ut_ref, um_ref, m_ref, r_ref, o_ref, pw_ref, dcol_ref, yt_ref):
    ng, th, rows = ut_ref.shape
    p = r_ref.shape[1] // 2
    lw, sl = LANES_V7X, SUBLANES_V7X
    nb = rows // lw
    h = yt_ref.shape[1] // ng
    lane0 = lax.broadcasted_iota(jnp.int32, (th, lw), 1) == 0

    ys, zre, zim = [], [], []
    for g in range(ng):
        meta_blk = jnp.where(lane0, um_ref[g], 0.0).astype(BF16)
        ue = jnp.concatenate([ut_ref[g], meta_blk], axis=1)
        ys.append(_dot(m_ref[g], ut_ref[g]))
        z = _dot(r_ref[g], ue)
        zre.append(z[:p])
        zim.append(z[p:])
    zr = jnp.concatenate(zre, axis=0)
    zi = jnp.concatenate(zim, axis=0)

    def chunk_major(z):
        return jnp.concatenate([z[:, b * lw:(b + 1) * lw].T for b in range(nb)], axis=1)

    def tile_lanes(v):
        return jnp.concatenate([v] * nb, axis=1)

    er, ei = chunk_major(zr), chunk_major(zi)
    car_r = tile_lanes(zr[:, rows:rows + lw].T[0:1, :])
    car_i = tile_lanes(zi[:, rows:rows + lw].T[0:1, :])
    pw = tile_lanes(pw_ref[...])
    nc, width = er.shape
    srow = lax.broadcasted_iota(jnp.int32, (nc, width), 0) % sl

    def tile_roll(x, sh):
        return pltpu.roll(x.reshape(nc // sl, sl, width), sh, 1).reshape(nc, width)

    for sh in (1, 2, 4):
        dr, di = _cmul(pw[sh - 1:sh], pw[sl + sh - 1:sl + sh], tile_roll(er, sh), tile_roll(ei, sh))
        ok = srow >= sh
        er = er + jnp.where(ok, dr, 0.0)
        ei = ei + jnp.where(ok, di, 0.0)

    a8r, a8i = pw[0:sl], pw[sl:2 * sl]
    first = lax.broadcasted_iota(jnp.int32, (sl, width), 0) == 0
    xr_rows, xi_rows = [], []
    for r in range(nc // sl):
        cr, ci = _cmul(a8r, a8i, car_r, car_i)
        fr = er[r * sl:(r + 1) * sl] + cr
        fi = ei[r * sl:(r + 1) * sl] + ci
        xr_rows.append(jnp.where(first, car_r, pltpu.roll(fr, 1, 0)))
        xi_rows.append(jnp.where(first, car_i, pltpu.roll(fi, 1, 0)))
        car_r, car_i = fr[sl - 1:sl], fi[sl - 1:sl]
    sr = jnp.concatenate(xr_rows, axis=0)
    si = jnp.concatenate(xi_rows, axis=0)

    def state_major(s):
        return jnp.concatenate([s[:, b * lw:(b + 1) * lw].T for b in range(nb)], axis=1)

    srt, sit = state_major(sr), state_major(si)
    for g in range(ng):
        s_in = jnp.concatenate([srt[g * p:(g + 1) * p], sit[g * p:(g + 1) * p]], axis=0).astype(BF16)
        y = ys[g] + _dot(o_ref[g], s_in) + dcol_ref[g] * ut_ref[g].astype(F32)
        yt_ref[:, g * h:(g + 1) * h, :] = y.reshape(th // h, h, rows).astype(BF16)


def _conv_branch_kernel(a_ref, halo_ref, am_ref, zc_ref, gc_ref, w_ref, b_ref, lng_ref, lnb_ref,
                        wc_ref, o_ref, s1_ref, s2_ref, yc_ref, cv_ref, *, width, ranges_per_seq):
    t, nr, e = a_ref.shape
    hr = halo_ref.shape[1]
    d = o_ref.shape[-1]
    seq_start = (pl.program_id(0) % ranges_per_seq) == 0
    for j in range(t):
        p1 = jnp.where(seq_start, am_ref[j:j + 1, :], halo_ref[j, hr - 1:hr, :])
        p2 = jnp.where(seq_start, 0.0, halo_ref[j, hr - 2:hr - 1, :])
        s1_ref[j, 0:1, :] = p1
        s1_ref[j, 1:nr, :] = a_ref[j, 0:nr - 1, :]
        s2_ref[j, 0:1, :] = p2
        s2_ref[j, 1:2, :] = p1
        s2_ref[j, 2:nr, :] = a_ref[j, 0:nr - 2, :]
    srcs = (a_ref, s1_ref, s2_ref)
    lng, lnb = lng_ref[...], lnb_ref[...]
    lw = LANES_V7X
    sl = w_ref.shape[1]
    for c0 in range(0, e, lw):
        wk = [w_ref[k, :, c0:c0 + lw] for k in range(width)]
        bias = jnp.broadcast_to(b_ref[:, c0:c0 + lw], (sl, lw))

        def row_tile(i, carry, c0=c0, wk=wk, bias=bias):
            r0 = pl.multiple_of(i * sl, sl)
            for jo in range(t):
                acc = bias
                for s in range(width):
                    r, q = s % t, s // t
                    src = srcs[q + (1 if r > jo else 0)]
                    acc = acc + wk[width - 1 - s] * src[(jo - r) % t, pl.ds(r0, sl), c0:c0 + lw]
                cv_ref[jo, pl.ds(r0, sl), c0:c0 + lw] = acc
            return carry

        lax.fori_loop(0, nr // sl, row_tile, 0)
    for jo in range(t):
        acc = cv_ref[jo]
        mu = jnp.mean(acc, axis=-1, keepdims=True)
        xc = acc - mu
        var = jnp.mean(xc * xc, axis=-1, keepdims=True)
        yn = xc * lax.rsqrt(var + LN_EPS) * lng + lnb
        yc_ref[jo * nr:(jo + 1) * nr, :] = (_silu(yn) * zc_ref[jo].astype(F32)).astype(BF16)
    pc = _dot(yc_ref[...], wc_ref[...])
    gate = gc_ref[...].reshape(t * nr, d).astype(F32)
    o_ref[...] = (gate * pc).astype(BF16).reshape(t, nr, d)


def _ssm_out_kernel(yt_ref, zt_ref, mc_ref, gs_ref, wg_ref, bcol_ref, ws_ref, w_ref, gain_ref,
                    x_hbm, o_hbm, xbuf, obuf, sem_in, sem_out):
    j, r = pl.program_id(0), pl.program_id(1)
    nr = pl.num_programs(1)
    step, n_steps = j * nr + r, pl.num_programs(0) * nr
    _, hb, nc, d = xbuf.shape
    slot = step % 2

    def x_copy(jj, rr, sl):
        return pltpu.make_async_copy(x_hbm.at[pl.ds(rr * hb, hb), :, jj, :], xbuf.at[sl], sem_in.at[sl])

    def o_copy(jj, rr, sl):
        return pltpu.make_async_copy(obuf.at[sl], o_hbm.at[pl.ds(rr * hb, hb), :, jj, :], sem_out.at[sl])

    @pl.when(step == 0)
    def _():
        x_copy(0, 0, 0).start()

    x_copy(j, r, slot).wait()

    @pl.when(step + 1 < n_steps)
    def _():
        nxt = step + 1
        x_copy(nxt // nr, nxt % nr, 1 - slot).start()

    @pl.when(step >= 2)
    def _():
        o_copy(j, r, slot).wait()

    y = jax.nn.gelu(yt_ref[...].astype(F32), approximate=True)
    g = _dot(wg_ref[...], y.astype(BF16)) + bcol_ref[...]
    y2t = (y * _sigmoid(g) * zt_ref[...].astype(F32)).astype(BF16)
    ps = _dot_tn(y2t, ws_ref[...])
    merged = (mc_ref[...].astype(F32) + gs_ref[...].astype(F32) * ps).astype(BF16)
    hres = xbuf[slot].reshape(hb * nc, d) + _dot(merged, w_ref[...])
    ms = jnp.mean(hres * hres, axis=-1, keepdims=True)
    obuf[slot] = (hres * lax.rsqrt(ms + NORM_EPS) * gain_ref[...]).reshape(hb, nc, d)
    o_copy(j, r, slot).start()

    @pl.when(step == n_steps - 1)
    def _():
        o_copy(j, r, 1 - slot).wait()
        o_copy(j, r, slot).wait()


def _col_tile(n):
    return 512 if n % 512 == 0 else n


def kernel(x, meta, norm_g, w_in, b_gate, dw_w, dw_b, ln_g, ln_b, w_conv, lam_re, lam_im,
           log_step, b_re, b_im, c_re, c_im, d_skip, w_glu, b_glu, w_ssm, w_out, final_g):
    bsz, seq, d = x.shape
    t = T_CHUNK
    e = w_conv.shape[0]
    g_n, p_n = lam_re.shape
    h_n = e // g_n
    width = dw_w.shape[0]
    nc = seq // t
    rows = bsz * nc
    assert meta.shape[0] == t and h_n * t == 2 * LANES_V7X and nc == LANES_V7X
    assert width <= 2 * t - 1 and seq % t == 0 and 2 * p_n == LANES_V7X and g_n % 2 == 0
    tn = _col_tile(e)
    ne = e // tn

    x4 = x.reshape(bsz, nc, t, d)
    w_in_b = w_in.astype(BF16)
    gain = norm_g.reshape(1, d)

    assert 2 * d == N_GATE_BLOCKS * e and w_in.shape[1] == PROJ_STEPS * e
    hbm = pl.BlockSpec(memory_space=pl.ANY)
    a_slab, zc_slab, gates, ut, zst = pl.pallas_call(
        _proj_kernel,
        grid=(t,),
        in_specs=[hbm, hbm,
                  pl.BlockSpec((1, d), lambda j: (0, 0)),
                  pl.BlockSpec((1, 2 * d), lambda j: (0, 0))],
        out_specs=[hbm] * 5,
        out_shape=[jax.ShapeDtypeStruct((t, rows, e), F32),
                   jax.ShapeDtypeStruct((t, rows, e), BF16),
                   jax.ShapeDtypeStruct((t, rows, 2 * d), BF16),
                   jax.ShapeDtypeStruct((g_n, t * h_n, rows), BF16),
                   jax.ShapeDtypeStruct((t, e, rows), BF16)],
        scratch_shapes=[pltpu.VMEM((bsz, nc, d), F32), pltpu.VMEM((2, rows, d), BF16),
                        pltpu.VMEM((2, d, e), BF16),
                        pltpu.VMEM((rows, e), F32), pltpu.VMEM((rows, e), BF16),
                        pltpu.VMEM((2, rows, e), BF16), pltpu.VMEM((g_n, h_n, rows), BF16),
                        pltpu.VMEM((e, rows), BF16),
                        pltpu.SemaphoreType.DMA((1,)), pltpu.SemaphoreType.DMA((2,)),
                        pltpu.SemaphoreType.DMA((6,))],
        compiler_params=_cparams("arbitrary"),
        name="proj",
    )(x4, w_in_b, gain, b_gate.reshape(1, 2 * d))

    a_meta, u_meta = pl.pallas_call(
        _proj_meta_kernel,
        grid=(ne,),
        in_specs=[pl.BlockSpec((t, d), lambda n: (0, 0)), pl.BlockSpec((1, d), lambda n: (0, 0)),
                  pl.BlockSpec((d, tn), lambda n: (0, n)), pl.BlockSpec((d, tn), lambda n: (0, ne + n)),
                  pl.BlockSpec((d, tn), lambda n: (0, 3 * ne + n))],
        out_specs=[pl.BlockSpec((t, tn), lambda n: (0, n)), pl.BlockSpec((t, tn), lambda n: (0, n))],
        out_shape=[jax.ShapeDtypeStruct((t, e), F32), jax.ShapeDtypeStruct((t, e), F32)],
        compiler_params=_cparams("arbitrary"),
        name="proj_meta",
    )(meta, gain, w_in_b, w_in_b, w_in_b)
    um_col = u_meta.reshape(t, g_n, h_n).transpose(1, 0, 2).reshape(g_n, t * h_n, 1)

    grp = lambda *shape: pl.BlockSpec((None,) + shape, lambda g: (g,) + (0,) * len(shape))
    tile2 = lambda v: jnp.concatenate([v, v], axis=-1)
    gpp = 8
    gblk = lambda *shape: pl.BlockSpec((gpp,) + shape, lambda g: (g,) + (0,) * len(shape))
    m_op, r_op, o_op, a_pow = pl.pallas_call(
        _s5_param_kernel,
        grid=(g_n // gpp,),
        in_specs=[gblk(p_n, 1), gblk(p_n, 1), gblk(1, 2 * p_n), gblk(1, 2 * p_n), gblk(1, 1),
                  gblk(p_n, t * h_n), gblk(p_n, t * h_n), gblk(h_n, 2 * p_n), gblk(h_n, 2 * p_n)],
        out_specs=[gblk(t * h_n, t * h_n), gblk(2 * p_n, t * h_n), gblk(t * h_n, 2 * p_n),
                   gblk(2 * SUBLANES_V7X, 2 * p_n)],
        out_shape=[jax.ShapeDtypeStruct((g_n, t * h_n, t * h_n), BF16),
                   jax.ShapeDtypeStruct((g_n, 2 * p_n, t * h_n), BF16),
                   jax.ShapeDtypeStruct((g_n, t * h_n, 2 * p_n), BF16),
                   jax.ShapeDtypeStruct((g_n, 2 * SUBLANES_V7X, 2 * p_n), F32)],
        compiler_params=_cparams("parallel"),
        name="s5_params",
    )(lam_re.reshape(g_n, p_n, 1), lam_im.reshape(g_n, p_n, 1),
      tile2(lam_re).reshape(g_n, 1, 2 * p_n), tile2(lam_im).reshape(g_n, 1, 2 * p_n),
      log_step.reshape(g_n, 1, 1),
      jnp.tile(b_re, (1, 1, t)), jnp.tile(b_im, (1, 1, t)),
      jnp.concatenate([c_re, c_im], axis=-1), jnp.concatenate([c_im, c_re], axis=-1))
    pw_pair = (a_pow[:, :, :p_n].reshape(g_n // 2, 2, 2 * SUBLANES_V7X, p_n)
               .transpose(0, 2, 1, 3).reshape(g_n // 2, 2 * SUBLANES_V7X, 2 * p_n))

    dcol = jnp.tile(d_skip.reshape(g_n, 1, h_n), (1, t, 1)).reshape(g_n, t * h_n, 1)
    pair = lambda *shape: pl.BlockSpec((2,) + shape, lambda g: (g,) + (0,) * len(shape))
    yt = pl.pallas_call(
        _s5_scan_kernel,
        grid=(g_n // 2,),
        in_specs=[pair(t * h_n, rows), pair(t * h_n, 1), pair(t * h_n, t * h_n), pair(2 * p_n, t * h_n),
                  pair(t * h_n, 2 * p_n), grp(2 * SUBLANES_V7X, 2 * p_n), pair(t * h_n, 1)],
        out_specs=pl.BlockSpec((t, 2 * h_n, rows), lambda g: (0, g, 0)),
        out_shape=jax.ShapeDtypeStruct((t, e, rows), BF16),
        compiler_params=_cparams("parallel"),
        name="s5_scan",
    )(ut, um_col, m_op, r_op, o_op, pw_pair, dcol)

    nr = 32
    hr = SUBLANES_V7X
    ranges_per_seq = nc // nr
    mc = pl.pallas_call(
        functools.partial(_conv_branch_kernel, width=width, ranges_per_seq=ranges_per_seq),
        grid=(rows // nr,),
        in_specs=[pl.BlockSpec((t, nr, e), lambda q: (0, q, 0)),
                  pl.BlockSpec((t, hr, e), lambda q: (0, jnp.maximum(q * (nr // hr) - 1, 0), 0)),
                  pl.BlockSpec((t, e), lambda q: (0, 0)),
                  pl.BlockSpec((t, nr, e), lambda q: (0, q, 0)),
                  pl.BlockSpec((t, nr, d), lambda q: (0, q, 0)),
                  pl.BlockSpec((width, SUBLANES_V7X, e), lambda q: (0, 0, 0)),
                  pl.BlockSpec((1, e), lambda q: (0, 0)),
                  pl.BlockSpec((1, e), lambda q: (0, 0)),
                  pl.BlockSpec((1, e), lambda q: (0, 0)),
                  pl.BlockSpec((e, d), lambda q: (0, 0))],
        out_specs=pl.BlockSpec((t, nr, d), lambda q: (0, q, 0)),
        out_shape=jax.ShapeDtypeStruct((t, rows, d), BF16),
        scratch_shapes=[pltpu.VMEM((t, nr, e), F32), pltpu.VMEM((t, nr, e), F32),
                        pltpu.VMEM((t * nr, e), BF16), pltpu.VMEM((t, nr, e), F32)],
        compiler_params=_cparams("parallel"),
        name="conv_branch",
    )(a_slab, a_slab, a_meta, zc_slab, gates,
      jnp.broadcast_to(dw_w.reshape(width, 1, e), (width, SUBLANES_V7X, e)), dw_b.reshape(1, e),
      ln_g.reshape(1, e), ln_b.reshape(1, e), w_conv.astype(BF16))

    rt = rows // 2
    hb = bsz // 2
    const = lambda *shape: pl.BlockSpec(shape, lambda j, r: (0,) * len(shape), pipeline_mode=pl.Buffered(1))
    out = pl.pallas_call(
        _ssm_out_kernel,
        grid=(t, 2),
        in_specs=[pl.BlockSpec((None, e, rt), lambda j, r: (j, 0, r)),
                  pl.BlockSpec((None, e, rt), lambda j, r: (j, 0, r)),
                  pl.BlockSpec((None, rt, d), lambda j, r: (j, r, 0)),
                  pl.BlockSpec((None, rt, d), lambda j, r: (j, r, 1)),
                  const(e, e), const(e, 1), const(e, d), const(d, d), const(1, d),
                  pl.BlockSpec(memory_space=pl.ANY)],
        out_specs=pl.BlockSpec(memory_space=pl.ANY),
        out_shape=jax.ShapeDtypeStruct((bsz, nc, t, d), F32),
        scratch_shapes=[pltpu.VMEM((2, hb, nc, d), F32), pltpu.VMEM((2, hb, nc, d), F32),
                        pltpu.SemaphoreType.DMA((2,)), pltpu.SemaphoreType.DMA((2,))],
        compiler_params=_cparams("arbitrary", "arbitrary"),
        name="ssm_out",
    )(yt, zst, mc, gates, w_glu.T.astype(BF16), b_glu.reshape(e, 1), w_ssm.astype(BF16),
      w_out.astype(BF16), final_g.reshape(1, d), x4)
    return out.reshape(bsz, seq, d)
```

```python
import functools
import math

import jax
import jax.numpy as jnp
from jax import lax
from jax.experimental import pallas as pl
from jax.experimental.pallas import tpu as pltpu

F32 = jnp.float32
BF16 = jnp.bfloat16

T_CHUNK = 16
LANES_V7X = 128
SUBLANES_V7X = 8
NORM_EPS = 1e-6
LN_EPS = 1e-5
LAM_RE_MAX = -1e-4
VMEM_LIMIT_V7X = 56 * 1024 * 1024


def _cparams(*sem):
    return pltpu.CompilerParams(dimension_semantics=sem, vmem_limit_bytes=VMEM_LIMIT_V7X)


def _sigmoid(x):
    return 0.5 * jnp.tanh(0.5 * x) + 0.5


def _silu(x):
    return x * _sigmoid(x)


def _dot(a, b):
    return jnp.dot(a, b, preferred_element_type=F32)


def _dot_nt(a, b):
    return lax.dot_general(a, b, (((1,), (1,)), ((), ())), preferred_element_type=F32)


def _dot_tn(a, b):
    return lax.dot_general(a, b, (((0,), (0,)), ((), ())), preferred_element_type=F32)


def _dot_tt(a, b):
    return lax.dot_general(a, b, (((0,), (1,)), ((), ())), preferred_element_type=F32)


def _cmul(ar, ai, br, bi):
    return ar * br - ai * bi, ar * bi + ai * br


def _norm_slab(x_ref, gain_ref, xn_ref):
    nb, nc, _ = x_ref.shape
    gain = gain_ref[...]

    def body(b, carry):
        x = x_ref[b]
        ms = jnp.mean(x * x, axis=-1, keepdims=True)
        r0 = pl.multiple_of(b * nc, nc)
        xn_ref[pl.ds(r0, nc), :] = (x * lax.rsqrt(ms + NORM_EPS) * gain).astype(BF16)
        return carry

    lax.fori_loop(0, nb, body, 0)


PROJ_STEPS = 9
N_WSLOTS = 3
PROJ_SECTION = (1, 0, 2, 5, 6, 7, 8, 3, 4)
STEP_G, STEP_V, STEP_ZC, STEP_GATE0, STEP_U, STEP_ZS = 0, 1, 2, 3, 7, 8
STEP_NORM_NEXT = 4
N_GATE_BLOCKS = STEP_U - STEP_GATE0


def _proj_kernel(x_hbm, w_hbm, gain_ref, bg_ref, a_hbm, zc_hbm, gt_hbm, ut_hbm, zt_hbm,
                 xbuf, xn_ref, wbuf, a_st, zc_st, gt_st, ut_st, zt_st, sem_x, sem_w, sem_o):
    j, nj = pl.program_id(0), pl.num_programs(0)
    e = wbuf.shape[2]
    h = ut_st.shape[1]
    cur = j % 2
    not_first = j > 0

    def x_copy(jj):
        return pltpu.make_async_copy(x_hbm.at[:, :, jj, :], xbuf, sem_x.at[0])

    def w_copy(k, slot):
        return pltpu.make_async_copy(w_hbm.at[:, pl.ds(PROJ_SECTION[k] * e, e)], wbuf.at[slot], sem_w.at[slot])

    def a_copy():
        return pltpu.make_async_copy(a_st, a_hbm.at[j], sem_o.at[0])

    def zc_copy():
        return pltpu.make_async_copy(zc_st, zc_hbm.at[j], sem_o.at[1])

    def gt_copy(g):
        return pltpu.make_async_copy(gt_st.at[g % 2], gt_hbm.at[j, :, pl.ds(g * e, e)], sem_o.at[2 + g % 2])

    def ut_copy():
        return pltpu.make_async_copy(ut_st, ut_hbm.at[:, pl.ds(j * h, h), :], sem_o.at[4])

    def zt_copy():
        return pltpu.make_async_copy(zt_st, zt_hbm.at[j], sem_o.at[5])

    def wait_previous(copy):
        @pl.when(not_first)
        def _():
            copy.wait()

    @pl.when(j == 0)
    def _():
        x_copy(0).start()
        for k in range(N_WSLOTS - 1):
            w_copy(k, k).start()
        x_copy(0).wait()
        _norm_slab(xbuf, gain_ref, xn_ref.at[0])

        @pl.when(nj > 1)
        def _():
            x_copy(1).start()

    xn = xn_ref.at[cur]

    def free_staging(k):
        if k == STEP_G:
            wait_previous(a_copy())
        elif k == STEP_ZC:
            wait_previous(zc_copy())
        elif STEP_GATE0 <= k < STEP_U:
            g = k - STEP_GATE0
            if g < 2:
                wait_previous(gt_copy(g))
            else:
                gt_copy(g).wait()
        elif k == STEP_U:
            wait_previous(ut_copy())
        elif k == STEP_ZS:
            wait_previous(zt_copy())

    def finish(k, acc):
        if k == STEP_G:
            a_st[...] = _sigmoid(acc)
        elif k == STEP_V:
            a_st[...] = a_st[...] * acc
            a_copy().start()
        elif k == STEP_ZC:
            zc_st[...] = _silu(acc).astype(BF16)
            zc_copy().start()
        elif k < STEP_U:
            g = k - STEP_GATE0
            gt_st[g % 2] = _sigmoid(acc + bg_ref[:, g * e:(g + 1) * e]).astype(BF16)
            gt_copy(g).start()
        elif k == STEP_U:
            ut_st[...] = acc.reshape(ut_st.shape).astype(BF16)
            ut_copy().start()
        else:
            zt_st[...] = _silu(acc).astype(BF16)
            zt_copy().start()

    for k in range(PROJ_STEPS):
        slot = k % N_WSLOTS
        w_copy(k, slot).wait()
        ahead = k + N_WSLOTS - 1
        if ahead < PROJ_STEPS:
            w_copy(ahead, ahead % N_WSLOTS).start()
        else:
            @pl.when(j + 1 < nj)
            def _(ahead=ahead):
                w_copy(ahead - PROJ_STEPS, ahead % N_WSLOTS).start()
        free_staging(k)
        w = wbuf[slot]
        finish(k, _dot_tt(w, xn[...]) if k >= STEP_U else _dot(xn[...], w))
        if k == STEP_NORM_NEXT:
            @pl.when(j + 1 < nj)
            def _():
                x_copy(j + 1).wait()
                _norm_slab(xbuf, gain_ref, xn_ref.at[1 - cur])

                @pl.when(j + 2 < nj)
                def _():
                    x_copy(j + 2).start()

    @pl.when(j == nj - 1)
    def _():
        a_copy().wait()
        zc_copy().wait()
        gt_copy(N_GATE_BLOCKS - 2).wait()
        gt_copy(N_GATE_BLOCKS - 1).wait()
        ut_copy().wait()
        zt_copy().wait()


def _proj_meta_kernel(m_ref, gain_ref, wv_ref, wg_ref, wu_ref, a_ref, u_ref):
    x = m_ref[...]
    ms = jnp.mean(x * x, axis=-1, keepdims=True)
    xn = (x * lax.rsqrt(ms + NORM_EPS) * gain_ref[...]).astype(BF16)
    a_ref[...] = _dot(xn, wv_ref[...]) * _sigmoid(_dot(xn, wg_ref[...]))
    u_ref[...] = _dot(xn, wu_ref[...])


def _lam_bar(lr, li, dt):
    lr = jnp.minimum(lr, LAM_RE_MAX)
    mag = jnp.exp(lr * dt)
    th = li * dt
    return lr, mag * jnp.cos(th), mag * jnp.sin(th)


def _s5_param_kernel(*refs):
    for g in range(refs[0].shape[0]):
        _s5_param_group(*[ref.at[g] for ref in refs])


def _s5_param_group(lrc_ref, lic_ref, lrr_ref, lir_ref, ls_ref, btr_ref, bti_ref,
                    ca_ref, cb_ref, m_ref, r_ref, o_ref, pw_ref):
    p = lrc_ref.shape[0]
    t = T_CHUNK
    h = btr_ref.shape[1]
    dt = jnp.exp(ls_ref[...])

    sel = (lax.broadcasted_iota(jnp.int32, (h, t * h), 1) % h
           == lax.broadcasted_iota(jnp.int32, (h, t * h), 0)).astype(F32)
    tile_b = lambda ref: jnp.dot(ref[...], sel, preferred_element_type=F32, precision=lax.Precision.HIGHEST)

    lr, lam_r, lam_i = _lam_bar(lrc_ref[...], lic_ref[...], dt)
    li = lic_ref[...]
    den = lr * lr + li * li
    nr, ni = lam_r - 1.0, lam_i
    kr = (nr * lr + ni * li) / den
    ki = (ni * lr - nr * li) / den
    bbr, bbi = _cmul(kr, ki, tile_b(btr_ref), tile_b(bti_ref))

    lane = lax.broadcasted_iota(jnp.int32, (p, t * h), 1)
    expo = (t - 1) - lane // h
    pr = jnp.ones((p, t * h), F32)
    pi = jnp.zeros((p, t * h), F32)
    br_, bi_ = lam_r, lam_i
    for k in range(int(math.log2(t))):
        qr, qi = _cmul(pr, pi, br_, bi_)
        bit = ((expo >> k) & 1) == 1
        pr = jnp.where(bit, qr, pr)
        pi = jnp.where(bit, qi, pi)
        br_, bi_ = _cmul(br_, bi_, br_, bi_)
    wr, wi = _cmul(pr, pi, bbr, bbi)
    r_ref[0:p, :] = wr.astype(BF16)
    r_ref[p:2 * p, :] = wi.astype(BF16)

    ca = ca_ref[...]
    cb = cb_ref[...]
    sgn = jnp.where(lax.broadcasted_iota(jnp.int32, ca.shape, 1) < p, 1.0, -1.0)
    w = jnp.concatenate([wr, wi], axis=0)
    krev = jnp.dot(sgn * ca, w, preferred_element_type=F32, precision=lax.Precision.HIGHEST)

    half = LANES_V7X
    ka, kb = krev[:, :half], krev[:, half:]
    lane_h = lax.broadcasted_iota(jnp.int32, (h, half), 1)
    zero = jnp.zeros((h, half), F32)
    for i in range(t):
        s = (t - 1 - i) * h
        if s == 0:
            lo, hi = ka, kb
        elif s < half:
            ra = pltpu.roll(ka, half - s, 1)
            rb = pltpu.roll(kb, half - s, 1)
            keep = lane_h < (half - s)
            lo, hi = jnp.where(keep, ra, rb), jnp.where(keep, rb, zero)
        elif s == half:
            lo, hi = kb, zero
        else:
            rb = pltpu.roll(kb, 2 * half - s, 1)
            lo, hi = jnp.where(lane_h < (2 * half - s), rb, zero), zero
        m_ref[i * h:(i + 1) * h, 0:half] = lo.astype(BF16)
        m_ref[i * h:(i + 1) * h, half:2 * half] = hi.astype(BF16)

    _, l1r, l1i = _lam_bar(lrr_ref[...], lir_ref[...], dt)
    qr, qi = l1r, l1i
    for i in range(t):
        o_ref[i * h:(i + 1) * h, :] = (sgn * (ca * qr) - cb * qi).astype(BF16)
        qr, qi = _cmul(qr, qi, l1r, l1i)

    ar, ai = l1r, l1i
    for _ in range(int(math.log2(t))):
        ar, ai = _cmul(ar, ai, ar, ai)
    qr, qi = ar, ai
    for s in range(SUBLANES_V7X):
        pw_ref[s:s + 1, :] = qr
        pw_ref[SUBLANES_V7X + s:SUBLANES_V7X + s + 1, :] = qi
        qr, qi = _cmul(qr, qi, ar, ai)


def _s5_scan_kernel(ut_ref, um_ref, m_ref, r_ref, o_ref, pw_ref, dcol_ref, yt_ref):
    ng, th, rows = ut_ref.shape
    p = r_ref.shape[1] // 2
    lw, sl = LANES_V7X, SUBLANES_V7X
    nb = rows // lw
    h = yt_ref.shape[1] // ng
    lane0 = lax.broadcasted_iota(jnp.int32, (th, lw), 1) == 0

    ys, zre, zim = [], [], []
    for g in range(ng):
        meta_blk = jnp.where(lane0, um_ref[g], 0.0).astype(BF16)
        ue = jnp.concatenate([ut_ref[g], meta_blk], axis=1)
        ys.append(_dot(m_ref[g], ut_ref[g]))
        z = _dot(r_ref[g], ue)
        zre.append(z[:p])
        zim.append(z[p:])
    zr = jnp.concatenate(zre, axis=0)
    zi = jnp.concatenate(zim, axis=0)

    def chunk_major(z):
        return jnp.concatenate([z[:, b * lw:(b + 1) * lw].T for b in range(nb)], axis=1)

    def tile_lanes(v):
        return jnp.concatenate([v] * nb, axis=1)

    er, ei = chunk_major(zr), chunk_major(zi)
    car_r = tile_lanes(zr[:, rows:rows + lw].T[0:1, :])
    car_i = tile_lanes(zi[:, rows:rows + lw].T[0:1, :])
    pw = tile_lanes(pw_ref[...])
    nc, width = er.shape
    srow = lax.broadcasted_iota(jnp.int32, (nc, width), 0) % sl

    def tile_roll(x, sh):
        return pltpu.roll(x.reshape(nc // sl, sl, width), sh, 1).reshape(nc, width)

    for sh in (1, 2, 4):
        dr, di = _cmul(pw[sh - 1:sh], pw[sl + sh - 1:sl + sh], tile_roll(er, sh), tile_roll(ei, sh))
        ok = srow >= sh
        er = er + jnp.where(ok, dr, 0.0)
        ei = ei + jnp.where(ok, di, 0.0)

    a8r, a8i = pw[0:sl], pw[sl:2 * sl]
    first = lax.broadcasted_iota(jnp.int32, (sl, width), 0) == 0
    xr_rows, xi_rows = [], []
    for r in range(nc // sl):
        cr, ci = _cmul(a8r, a8i, car_r, car_i)
        fr = er[r * sl:(r + 1) * sl] + cr
        fi = ei[r * sl:(r + 1) * sl] + ci
        xr_rows.append(jnp.where(first, car_r, pltpu.roll(fr, 1, 0)))
        xi_rows.append(jnp.where(first, car_i, pltpu.roll(fi, 1, 0)))
        car_r, car_i = fr[sl - 1:sl], fi[sl - 1:sl]
    sr = jnp.concatenate(xr_rows, axis=0)
    si = jnp.concatenate(xi_rows, axis=0)

    def state_major(s):
        return jnp.concatenate([s[:, b * lw:(b + 1) * lw].T for b in range(nb)], axis=1)

    srt, sit = state_major(sr), state_major(si)
    for g in range(ng):
        s_in = jnp.concatenate([srt[g * p:(g + 1) * p], sit[g * p:(g + 1) * p]], axis=0).astype(BF16)
        y = ys[g] + _dot(o_ref[g], s_in) + dcol_ref[g] * ut_ref[g].astype(F32)
        yt_ref[:, g * h:(g + 1) * h, :] = y.reshape(th // h, h, rows).astype(BF16)


def _conv_branch_kernel(a_ref, halo_ref, am_ref, zc_ref, gc_ref, w_ref, b_ref, lng_ref, lnb_ref,
                        wc_ref, o_ref, s1_ref, s2_ref, yc_ref, cv_ref, *, width, ranges_per_seq):
    t, nr, e = a_ref.shape
    hr = halo_ref.shape[1]
    d = o_ref.shape[-1]
    seq_start = (pl.program_id(0) % ranges_per_seq) == 0
    for j in range(t):
        p1 = jnp.where(seq_start, am_ref[j:j + 1, :], halo_ref[j, hr - 1:hr, :])
        p2 = jnp.where(seq_start, 0.0, halo_ref[j, hr - 2:hr - 1, :])
        s1_ref[j, 0:1, :] = p1
        s1_ref[j, 1:nr, :] = a_ref[j, 0:nr - 1, :]
        s2_ref[j, 0:1, :] = p2
        s2_ref[j, 1:2, :] = p1
        s2_ref[j, 2:nr, :] = a_ref[j, 0:nr - 2, :]
    srcs = (a_ref, s1_ref, s2_ref)
    lng, lnb = lng_ref[...], lnb_ref[...]
    lw = LANES_V7X
    sl = w_ref.shape[1]
    for c0 in range(0, e, lw):
        wk = [w_ref[k, :, c0:c0 + lw] for k in range(width)]
        bias = jnp.broadcast_to(b_ref[:, c0:c0 + lw], (sl, lw))

        def row_tile(i, carry, c0=c0, wk=wk, bias=bias):
            r0 = pl.multiple_of(i * sl, sl)
            for jo in range(t):
                acc = bias
                for s in range(width):
                    r, q = s % t, s // t
                    src = srcs[q + (1 if r > jo else 0)]
                    acc = acc + wk[width - 1 - s] * src[(jo - r) % t, pl.ds(r0, sl), c0:c0 + lw]
                cv_ref[jo, pl.ds(r0, sl), c0:c0 + lw] = acc
            return carry

        lax.fori_loop(0, nr // sl, row_tile, 0)
    for jo in range(t):
        acc = cv_ref[jo]
        mu = jnp.mean(acc, axis=-1, keepdims=True)
        xc = acc - mu
        var = jnp.mean(xc * xc, axis=-1, keepdims=True)
        yn = xc * lax.rsqrt(var + LN_EPS) * lng + lnb
        yc_ref[jo * nr:(jo + 1) * nr, :] = (_silu(yn) * zc_ref[jo].astype(F32)).astype(BF16)
    pc = _dot(yc_ref[...], wc_ref[...])
    gate = gc_ref[...].reshape(t * nr, d).astype(F32)
    o_ref[...] = (gate * pc).astype(BF16).reshape(t, nr, d)


def _ssm_out_kernel(yt_ref, zt_ref, mc_ref, gs_ref, wg_ref, bcol_ref, ws_ref, w_ref, gain_ref,
                    x_hbm, o_hbm, xbuf, obuf, sem_in, sem_out):
    j, r = pl.program_id(0), pl.program_id(1)
    nr = pl.num_programs(1)
    step, n_steps = j * nr + r, pl.num_programs(0) * nr
    _, hb, nc, d = xbuf.shape
    slot = step % 2

    def x_copy(jj, rr, sl):
        return pltpu.make_async_copy(x_hbm.at[pl.ds(rr * hb, hb), :, jj, :], xbuf.at[sl], sem_in.at[sl])

    def o_copy(jj, rr, sl):
        return pltpu.make_async_copy(obuf.at[sl], o_hbm.at[pl.ds(rr * hb, hb), :, jj, :], sem_out.at[sl])

    @pl.when(step == 0)
    def _():
        x_copy(0, 0, 0).start()

    x_copy(j, r, slot).wait()

    @pl.when(step + 1 < n_steps)
    def _():
        nxt = step + 1
        x_copy(nxt // nr, nxt % nr, 1 - slot).start()

    @pl.when(step >= 2)
    def _():
        o_copy(j, r, slot).wait()

    y = jax.nn.gelu(yt_ref[...].astype(F32), approximate=True)
    g = _dot(wg_ref[...], y.astype(BF16)) + bcol_ref[...]
    y2t = (y * _sigmoid(g) * zt_ref[...].astype(F32)).astype(BF16)
    ps = _dot_tn(y2t, ws_ref[...])
    merged = (mc_ref[...].astype(F32) + gs_ref[...].astype(F32) * ps).astype(BF16)
    hres = xbuf[slot].reshape(hb * nc, d) + _dot(merged, w_ref[...])
    ms = jnp.mean(hres * hres, axis=-1, keepdims=True)
    obuf[slot] = (hres * lax.rsqrt(ms + NORM_EPS) * gain_ref[...]).reshape(hb, nc, d)
    o_copy(j, r, slot).start()

    @pl.when(step == n_steps - 1)
    def _():
        o_copy(j, r, 1 - slot).wait()
        o_copy(j, r, slot).wait()


def _col_tile(n):
    return 512 if n % 512 == 0 else n


def kernel(x, meta, norm_g, w_in, b_gate, dw_w, dw_b, ln_g, ln_b, w_conv, lam_re, lam_im,
           log_step, b_re, b_im, c_re, c_im, d_skip, w_glu, b_glu, w_ssm, w_out, final_g):
    bsz, seq, d = x.shape
    t = T_CHUNK
    e = w_conv.shape[0]
    g_n, p_n = lam_re.shape
    h_n = e // g_n
    width = dw_w.shape[0]
    nc = seq // t
    rows = bsz * nc
    assert meta.shape[0] == t and h_n * t == 2 * LANES_V7X and nc == LANES_V7X
    assert width <= 2 * t - 1 and seq % t == 0 and 2 * p_n == LANES_V7X and g_n % 2 == 0
    tn = _col_tile(e)
    ne = e // tn

    x4 = x.reshape(bsz, nc, t, d)
    w_in_b = w_in.astype(BF16)
    gain = norm_g.reshape(1, d)

    assert 2 * d == N_GATE_BLOCKS * e and w_in.shape[1] == PROJ_STEPS * e and PROJ_STEPS % N_WSLOTS == 0
    hbm = pl.BlockSpec(memory_space=pl.ANY)
    a_slab, zc_slab, gates, ut, zst = pl.pallas_call(
        _proj_kernel,
        grid=(t,),
        in_specs=[hbm, hbm,
                  pl.BlockSpec((1, d), lambda j: (0, 0)),
                  pl.BlockSpec((1, 2 * d), lambda j: (0, 0))],
        out_specs=[hbm] * 5,
        out_shape=[jax.ShapeDtypeStruct((t, rows, e), F32),
                   jax.ShapeDtypeStruct((t, rows, e), BF16),
                   jax.ShapeDtypeStruct((t, rows, 2 * d), BF16),
                   jax.ShapeDtypeStruct((g_n, t * h_n, rows), BF16),
                   jax.ShapeDtypeStruct((t, e, rows), BF16)],
        scratch_shapes=[pltpu.VMEM((bsz, nc, d), F32), pltpu.VMEM((2, rows, d), BF16),
                        pltpu.VMEM((N_WSLOTS, d, e), BF16),
                        pltpu.VMEM((rows, e), F32), pltpu.VMEM((rows, e), BF16),
                        pltpu.VMEM((2, rows, e), BF16), pltpu.VMEM((g_n, h_n, rows), BF16),
                        pltpu.VMEM((e, rows), BF16),
                        pltpu.SemaphoreType.DMA((1,)), pltpu.SemaphoreType.DMA((N_WSLOTS,)),
                        pltpu.SemaphoreType.DMA((6,))],
        compiler_params=_cparams("arbitrary"),
        name="proj",
    )(x4, w_in_b, gain, b_gate.reshape(1, 2 * d))

    a_meta, u_meta = pl.pallas_call(
        _proj_meta_kernel,
        grid=(ne,),
        in_specs=[pl.BlockSpec((t, d), lambda n: (0, 0)), pl.BlockSpec((1, d), lambda n: (0, 0)),
                  pl.BlockSpec((d, tn), lambda n: (0, n)), pl.BlockSpec((d, tn), lambda n: (0, ne + n)),
                  pl.BlockSpec((d, tn), lambda n: (0, 3 * ne + n))],
        out_specs=[pl.BlockSpec((t, tn), lambda n: (0, n)), pl.BlockSpec((t, tn), lambda n: (0, n))],
        out_shape=[jax.ShapeDtypeStruct((t, e), F32), jax.ShapeDtypeStruct((t, e), F32)],
        compiler_params=_cparams("arbitrary"),
        name="proj_meta",
    )(meta, gain, w_in_b, w_in_b, w_in_b)
    um_col = u_meta.reshape(t, g_n, h_n).transpose(1, 0, 2).reshape(g_n, t * h_n, 1)

    grp = lambda *shape: pl.BlockSpec((None,) + shape, lambda g: (g,) + (0,) * len(shape))
    tile2 = lambda v: jnp.concatenate([v, v], axis=-1)
    gpp = 8
    gblk = lambda *shape: pl.BlockSpec((gpp,) + shape, lambda g: (g,) + (0,) * len(shape))
    m_op, r_op, o_op, a_pow = pl.pallas_call(
        _s5_param_kernel,
        grid=(g_n // gpp,),
        in_specs=[gblk(p_n, 1), gblk(p_n, 1), gblk(1, 2 * p_n), gblk(1, 2 * p_n), gblk(1, 1),
                  gblk(p_n, h_n), gblk(p_n, h_n), gblk(h_n, 2 * p_n), gblk(h_n, 2 * p_n)],
        out_specs=[gblk(t * h_n, t * h_n), gblk(2 * p_n, t * h_n), gblk(t * h_n, 2 * p_n),
                   gblk(2 * SUBLANES_V7X, 2 * p_n)],
        out_shape=[jax.ShapeDtypeStruct((g_n, t * h_n, t * h_n), BF16),
                   jax.ShapeDtypeStruct((g_n, 2 * p_n, t * h_n), BF16),
                   jax.ShapeDtypeStruct((g_n, t * h_n, 2 * p_n), BF16),
                   jax.ShapeDtypeStruct((g_n, 2 * SUBLANES_V7X, 2 * p_n), F32)],
        compiler_params=_cparams("parallel"),
        name="s5_params",
    )(lam_re.reshape(g_n, p_n, 1), lam_im.reshape(g_n, p_n, 1),
      tile2(lam_re).reshape(g_n, 1, 2 * p_n), tile2(lam_im).reshape(g_n, 1, 2 * p_n),
      log_step.reshape(g_n, 1, 1),
      b_re, b_im,
      jnp.concatenate([c_re, c_im], axis=-1), jnp.concatenate([c_im, c_re], axis=-1))
    pw_pair = (a_pow[:, :, :p_n].reshape(g_n // 2, 2, 2 * SUBLANES_V7X, p_n)
               .transpose(0, 2, 1, 3).reshape(g_n // 2, 2 * SUBLANES_V7X, 2 * p_n))

    dcol = jnp.tile(d_skip.reshape(g_n, 1, h_n), (1, t, 1)).reshape(g_n, t * h_n, 1)
    pair = lambda *shape: pl.BlockSpec((2,) + shape, lambda g: (g,) + (0,) * len(shape))
    yt = pl.pallas_call(
        _s5_scan_kernel,
        grid=(g_n // 2,),
        in_specs=[pair(t * h_n, rows), pair(t * h_n, 1), pair(t * h_n, t * h_n), pair(2 * p_n, t * h_n),
                  pair(t * h_n, 2 * p_n), grp(2 * SUBLANES_V7X, 2 * p_n), pair(t * h_n, 1)],
        out_specs=pl.BlockSpec((t, 2 * h_n, rows), lambda g: (0, g, 0)),
        out_shape=jax.ShapeDtypeStruct((t, e, rows), BF16),
        compiler_params=_cparams("parallel"),
        name="s5_scan",
    )(ut, um_col, m_op, r_op, o_op, pw_pair, dcol)

    nr = 64
    hr = SUBLANES_V7X
    ranges_per_seq = nc // nr
    mc = pl.pallas_call(
        functools.partial(_conv_branch_kernel, width=width, ranges_per_seq=ranges_per_seq),
        grid=(rows // nr,),
        in_specs=[pl.BlockSpec((t, nr, e), lambda q: (0, q, 0)),
                  pl.BlockSpec((t, hr, e), lambda q: (0, jnp.maximum(q * (nr // hr) - 1, 0), 0)),
                  pl.BlockSpec((t, e), lambda q: (0, 0)),
                  pl.BlockSpec((t, nr, e), lambda q: (0, q, 0)),
                  pl.BlockSpec((t, nr, d), lambda q: (0, q, 0)),
                  pl.BlockSpec((width, SUBLANES_V7X, e), lambda q: (0, 0, 0)),
                  pl.BlockSpec((1, e), lambda q: (0, 0)),
                  pl.BlockSpec((1, e), lambda q: (0, 0)),
                  pl.BlockSpec((1, e), lambda q: (0, 0)),
                  pl.BlockSpec((e, d), lambda q: (0, 0))],
        out_specs=pl.BlockSpec((t, nr, d), lambda q: (0, q, 0)),
        out_shape=jax.ShapeDtypeStruct((t, rows, d), BF16),
        scratch_shapes=[pltpu.VMEM((t, nr, e), F32), pltpu.VMEM((t, nr, e), F32),
                        pltpu.VMEM((t * nr, e), BF16), pltpu.VMEM((t, nr, e), F32)],
        compiler_params=_cparams("parallel"),
        name="conv_branch",
    )(a_slab, a_slab, a_meta, zc_slab, gates,
      jnp.broadcast_to(dw_w.reshape(width, 1, e), (width, SUBLANES_V7X, e)), dw_b.reshape(1, e),
      ln_g.reshape(1, e), ln_b.reshape(1, e), w_conv.astype(BF16))

    rt = rows // 2
    hb = bsz // 2
    const = lambda *shape: pl.BlockSpec(shape, lambda j, r: (0,) * len(shape), pipeline_mode=pl.Buffered(1))
    out = pl.pallas_call(
        _ssm_out_kernel,
        grid=(t, 2),
        in_specs=[pl.BlockSpec((None, e, rt), lambda j, r: (j, 0, r)),
                  pl.BlockSpec((None, e, rt), lambda j, r: (j, 0, r)),
                  pl.BlockSpec((None, rt, d), lambda j, r: (j, r, 0)),
                  pl.BlockSpec((None, rt, d), lambda j, r: (j, r, 1)),
                  const(e, e), const(e, 1), const(e, d), const(d, d), const(1, d),
                  pl.BlockSpec(memory_space=pl.ANY)],
        out_specs=pl.BlockSpec(memory_space=pl.ANY),
        out_shape=jax.ShapeDtypeStruct((bsz, nc, t, d), F32),
        scratch_shapes=[pltpu.VMEM((2, hb, nc, d), F32), pltpu.VMEM((2, hb, nc, d), F32),
                        pltpu.SemaphoreType.DMA((2,)), pltpu.SemaphoreType.DMA((2,))],
        compiler_params=_cparams("arbitrary", "arbitrary"),
        name="ssm_out",
    )(yt, zst, mc, gates, w_glu.T.astype(BF16), b_glu.reshape(e, 1), w_ssm.astype(BF16),
      w_out.astype(BF16), final_g.reshape(1, d), x4)
    return out.reshape(bsz, seq, d)
```

```python
import functools
import math

import jax
import jax.numpy as jnp
from jax import lax
from jax.experimental import pallas as pl
from jax.experimental.pallas import tpu as pltpu

F32 = jnp.float32
BF16 = jnp.bfloat16

T_CHUNK = 16
LANES_V7X = 128
SUBLANES_V7X = 8
NORM_EPS = 1e-6
LN_EPS = 1e-5
LAM_RE_MAX = -1e-4
VMEM_LIMIT_V7X = 56 * 1024 * 1024


def _cparams(*sem):
    return pltpu.CompilerParams(dimension_semantics=sem, vmem_limit_bytes=VMEM_LIMIT_V7X)


def _sigmoid(x):
    return 0.5 * jnp.tanh(0.5 * x) + 0.5


def _silu(x):
    return x * _sigmoid(x)


def _dot(a, b):
    return jnp.dot(a, b, preferred_element_type=F32)


def _dot_nt(a, b):
    return lax.dot_general(a, b, (((1,), (1,)), ((), ())), preferred_element_type=F32)


def _dot_tn(a, b):
    return lax.dot_general(a, b, (((0,), (0,)), ((), ())), preferred_element_type=F32)


def _dot_tt(a, b):
    return lax.dot_general(a, b, (((0,), (1,)), ((), ())), preferred_element_type=F32)


def _cmul(ar, ai, br, bi):
    return ar * br - ai * bi, ar * bi + ai * br


def _norm_slab(x_ref, gain_ref, xn_ref):
    nb, nc, _ = x_ref.shape
    gain = gain_ref[...]

    def body(b, carry):
        x = x_ref[b]
        ms = jnp.mean(x * x, axis=-1, keepdims=True)
        r0 = pl.multiple_of(b * nc, nc)
        xn_ref[pl.ds(r0, nc), :] = (x * lax.rsqrt(ms + NORM_EPS) * gain).astype(BF16)
        return carry

    lax.fori_loop(0, nb, body, 0)


PROJ_STEPS = 9
N_WSLOTS = 3
PROJ_SECTION = (1, 0, 2, 5, 6, 7, 8, 3, 4)
STEP_G, STEP_V, STEP_ZC, STEP_GATE0, STEP_U, STEP_ZS = 0, 1, 2, 3, 7, 8
N_GATE_BLOCKS = STEP_U - STEP_GATE0


def _proj_kernel(x_hbm, wf_hbm, gain_ref, bg_ref, a_hbm, zc_hbm, gt_hbm, ut_hbm, zt_hbm, wb_hbm,
                 xbuf, xn_ref, wbuf, wf32, a_st, zc_st, gt_st, ut_st, zt_st,
                 sem_x, sem_w, sem_wf, sem_ws, sem_o):
    j, nj = pl.program_id(0), pl.num_programs(0)
    e = wbuf.shape[2]
    h = ut_st.shape[1]
    not_first = j > 0

    def x_copy(jj):
        return pltpu.make_async_copy(x_hbm.at[:, :, jj, :], xbuf, sem_x.at[0])

    def section(ref, k):
        return ref.at[:, pl.ds(PROJ_SECTION[k] * e, e)]

    def w_copy(k, slot):
        return pltpu.make_async_copy(section(wb_hbm, k), wbuf.at[slot], sem_w.at[slot])

    def wf_copy(k):
        return pltpu.make_async_copy(section(wf_hbm, k), wf32, sem_wf.at[0])

    def ws_copy(k, slot):
        return pltpu.make_async_copy(wbuf.at[slot], section(wb_hbm, k), sem_ws.at[slot])

    def a_copy():
        return pltpu.make_async_copy(a_st, a_hbm.at[j], sem_o.at[0])

    def zc_copy():
        return pltpu.make_async_copy(zc_st, zc_hbm.at[j], sem_o.at[1])

    def gt_copy(g):
        return pltpu.make_async_copy(gt_st.at[g % 2], gt_hbm.at[j, :, pl.ds(g * e, e)], sem_o.at[2 + g % 2])

    def ut_copy():
        return pltpu.make_async_copy(ut_st, ut_hbm.at[:, pl.ds(j * h, h), :], sem_o.at[4])

    def zt_copy():
        return pltpu.make_async_copy(zt_st, zt_hbm.at[j], sem_o.at[5])

    def wait_previous(copy):
        @pl.when(not_first)
        def _():
            copy.wait()

    @pl.when(j == 0)
    def _():
        x_copy(0).start()
        wf_copy(0).start()

    x_copy(j).wait()
    _norm_slab(xbuf, gain_ref, xn_ref)

    @pl.when(j + 1 < nj)
    def _():
        x_copy(j + 1).start()

    xn = xn_ref

    def cast_section(slot):
        rb = 256

        def body(i, carry):
            r0 = pl.multiple_of(i * rb, rb)
            wbuf[slot, pl.ds(r0, rb), :] = wf32[pl.ds(r0, rb), :].astype(BF16)
            return carry

        lax.fori_loop(0, wf32.shape[0] // rb, body, 0)

    def acquire(k):
        slot = k % N_WSLOTS

        @pl.when(j == 0)
        def _():
            wf_copy(k).wait()
            if k >= N_WSLOTS:
                ws_copy(k - N_WSLOTS, slot).wait()
            cast_section(slot)
            if k + 1 < PROJ_STEPS:
                wf_copy(k + 1).start()
            ws_copy(k, slot).start()

        @pl.when(not_first)
        def _():
            w_copy(k, slot).wait()
            ahead = k + N_WSLOTS - 1
            if ahead < PROJ_STEPS:
                w_copy(ahead, ahead % N_WSLOTS).start()
            else:
                @pl.when(j + 1 < nj)
                def _():
                    w_copy(ahead - PROJ_STEPS, ahead % N_WSLOTS).start()

    def free_staging(k):
        if k == STEP_G:
            wait_previous(a_copy())
        elif k == STEP_ZC:
            wait_previous(zc_copy())
        elif STEP_GATE0 <= k < STEP_U:
            g = k - STEP_GATE0
            if g < 2:
                wait_previous(gt_copy(g))
            else:
                gt_copy(g).wait()
        elif k == STEP_U:
            wait_previous(ut_copy())
        elif k == STEP_ZS:
            wait_previous(zt_copy())

    def finish(k, acc):
        if k == STEP_G:
            a_st[...] = _sigmoid(acc)
        elif k == STEP_V:
            a_st[...] = a_st[...] * acc
            a_copy().start()
        elif k == STEP_ZC:
            zc_st[...] = _silu(acc).astype(BF16)
            zc_copy().start()
        elif k < STEP_U:
            g = k - STEP_GATE0
            gt_st[g % 2] = _sigmoid(acc + bg_ref[:, g * e:(g + 1) * e]).astype(BF16)
            gt_copy(g).start()
        elif k == STEP_U:
            ut_st[...] = acc.reshape(ut_st.shape).astype(BF16)
            ut_copy().start()
        else:
            zt_st[...] = _silu(acc).astype(BF16)
            zt_copy().start()

    for k in range(PROJ_STEPS):
        acquire(k)
        free_staging(k)
        w = wbuf[k % N_WSLOTS]
        finish(k, _dot_tt(w, xn[...]) if k >= STEP_U else _dot(xn[...], w))

    @pl.when((j == 0) & (nj > 1))
    def _():
        for k in range(PROJ_STEPS - N_WSLOTS, PROJ_STEPS):
            ws_copy(k, k % N_WSLOTS).wait()
        for k in range(N_WSLOTS - 1):
            w_copy(k, k).start()

    @pl.when((j == 0) & (nj == 1))
    def _():
        for k in range(PROJ_STEPS - N_WSLOTS, PROJ_STEPS):
            ws_copy(k, k % N_WSLOTS).wait()

    @pl.when(j == nj - 1)
    def _():
        a_copy().wait()
        zc_copy().wait()
        gt_copy(N_GATE_BLOCKS - 2).wait()
        gt_copy(N_GATE_BLOCKS - 1).wait()
        ut_copy().wait()
        zt_copy().wait()


def _proj_meta_kernel(m_ref, gain_ref, wv_ref, wg_ref, wu_ref, a_ref, u_ref):
    x = m_ref[...]
    ms = jnp.mean(x * x, axis=-1, keepdims=True)
    xn = (x * lax.rsqrt(ms + NORM_EPS) * gain_ref[...]).astype(BF16)
    a_ref[...] = _dot(xn, wv_ref[...]) * _sigmoid(_dot(xn, wg_ref[...]))
    u_ref[...] = _dot(xn, wu_ref[...])


def _lam_bar(lr, li, dt):
    lr = jnp.minimum(lr, LAM_RE_MAX)
    mag = jnp.exp(lr * dt)
    th = li * dt
    return lr, mag * jnp.cos(th), mag * jnp.sin(th)


def _s5_param_kernel(*refs):
    for g in range(refs[0].shape[0]):
        _s5_param_group(*[ref.at[g] for ref in refs])


def _s5_param_group(lrc_ref, lic_ref, lrr_ref, lir_ref, ls_ref, btr_ref, bti_ref,
                    ca_ref, cb_ref, m_ref, r_ref, o_ref, pw_ref):
    p = lrc_ref.shape[0]
    t = T_CHUNK
    h = btr_ref.shape[1]
    dt = jnp.exp(ls_ref[...])

    sel = (lax.broadcasted_iota(jnp.int32, (h, t * h), 1) % h
           == lax.broadcasted_iota(jnp.int32, (h, t * h), 0)).astype(F32)
    tile_b = lambda ref: jnp.dot(ref[...], sel, preferred_element_type=F32, precision=lax.Precision.HIGHEST)

    lr, lam_r, lam_i = _lam_bar(lrc_ref[...], lic_ref[...], dt)
    li = lic_ref[...]
    den = lr * lr + li * li
    nr, ni = lam_r - 1.0, lam_i
    kr = (nr * lr + ni * li) / den
    ki = (ni * lr - nr * li) / den
    bbr, bbi = _cmul(kr, ki, tile_b(btr_ref), tile_b(bti_ref))

    lane = lax.broadcasted_iota(jnp.int32, (p, t * h), 1)
    expo = (t - 1) - lane // h
    pr = jnp.ones((p, t * h), F32)
    pi = jnp.zeros((p, t * h), F32)
    br_, bi_ = lam_r, lam_i
    for k in range(int(math.log2(t))):
        qr, qi = _cmul(pr, pi, br_, bi_)
        bit = ((expo >> k) & 1) == 1
        pr = jnp.where(bit, qr, pr)
        pi = jnp.where(bit, qi, pi)
        br_, bi_ = _cmul(br_, bi_, br_, bi_)
    wr, wi = _cmul(pr, pi, bbr, bbi)
    r_ref[0:p, :] = wr.astype(BF16)
    r_ref[p:2 * p, :] = wi.astype(BF16)

    ca = ca_ref[...]
    cb = cb_ref[...]
    sgn = jnp.where(lax.broadcasted_iota(jnp.int32, ca.shape, 1) < p, 1.0, -1.0)
    w = jnp.concatenate([wr, wi], axis=0)
    krev = jnp.dot(sgn * ca, w, preferred_element_type=F32, precision=lax.Precision.HIGHEST)

    half = LANES_V7X
    ka, kb = krev[:, :half], krev[:, half:]
    lane_h = lax.broadcasted_iota(jnp.int32, (h, half), 1)
    zero = jnp.zeros((h, half), F32)
    for i in range(t):
        s = (t - 1 - i) * h
        if s == 0:
            lo, hi = ka, kb
        elif s < half:
            ra = pltpu.roll(ka, half - s, 1)
            rb = pltpu.roll(kb, half - s, 1)
            keep = lane_h < (half - s)
            lo, hi = jnp.where(keep, ra, rb), jnp.where(keep, rb, zero)
        elif s == half:
            lo, hi = kb, zero
        else:
            rb = pltpu.roll(kb, 2 * half - s, 1)
            lo, hi = jnp.where(lane_h < (2 * half - s), rb, zero), zero
        m_ref[i * h:(i + 1) * h, 0:half] = lo.astype(BF16)
        m_ref[i * h:(i + 1) * h, half:2 * half] = hi.astype(BF16)

    _, l1r, l1i = _lam_bar(lrr_ref[...], lir_ref[...], dt)
    qr, qi = l1r, l1i
    for i in range(t):
        o_ref[i * h:(i + 1) * h, :] = (sgn * (ca * qr) - cb * qi).astype(BF16)
        qr, qi = _cmul(qr, qi, l1r, l1i)

    ar, ai = l1r, l1i
    for _ in range(int(math.log2(t))):
        ar, ai = _cmul(ar, ai, ar, ai)
    qr, qi = ar, ai
    for s in range(SUBLANES_V7X):
        pw_ref[s:s + 1, :] = qr
        pw_ref[SUBLANES_V7X + s:SUBLANES_V7X + s + 1, :] = qi
        qr, qi = _cmul(qr, qi, ar, ai)


def _s5_scan_kernel(ut_ref, um_ref, m_ref, r_ref, o_ref, pw_ref, dcol_ref, yt_ref):
    ng, th, rows = ut_ref.shape
    p = r_ref.shape[1] // 2
    lw, sl = LANES_V7X, SUBLANES_V7X
    nb = rows // lw
    h = yt_ref.shape[1] // ng
    lane0 = lax.broadcasted_iota(jnp.int32, (th, lw), 1) == 0

    ys, zre, zim = [], [], []
    for g in range(ng):
        meta_blk = jnp.where(lane0, um_ref[g], 0.0).astype(BF16)
        ue = jnp.concatenate([ut_ref[g], meta_blk], axis=1)
        ys.append(_dot(m_ref[g], ut_ref[g]))
        z = _dot(r_ref[g], ue)
        zre.append(z[:p])
        zim.append(z[p:])
    zr = jnp.concatenate(zre, axis=0)
    zi = jnp.concatenate(zim, axis=0)

    def chunk_major(z):
        return jnp.concatenate([z[:, b * lw:(b + 1) * lw].T for b in range(nb)], axis=1)

    def tile_lanes(v):
        return jnp.concatenate([v] * nb, axis=1)

    er, ei = chunk_major(zr), chunk_major(zi)
    car_r = tile_lanes(zr[:, rows:rows + lw].T[0:1, :])
    car_i = tile_lanes(zi[:, rows:rows + lw].T[0:1, :])
    pw = tile_lanes(pw_ref[...])
    nc, width = er.shape
    srow = lax.broadcasted_iota(jnp.int32, (nc, width), 0) % sl

    def tile_roll(x, sh):
        return pltpu.roll(x.reshape(nc // sl, sl, width), sh, 1).reshape(nc, width)

    for sh in (1, 2, 4):
        dr, di = _cmul(pw[sh - 1:sh], pw[sl + sh - 1:sl + sh], tile_roll(er, sh), tile_roll(ei, sh))
        ok = srow >= sh
        er = er + jnp.where(ok, dr, 0.0)
        ei = ei + jnp.where(ok, di, 0.0)

    a8r, a8i = pw[0:sl], pw[sl:2 * sl]
    first = lax.broadcasted_iota(jnp.int32, (sl, width), 0) == 0
    xr_rows, xi_rows = [], []
    for r in range(nc // sl):
        cr, ci = _cmul(a8r, a8i, car_r, car_i)
        fr = er[r * sl:(r + 1) * sl] + cr
        fi = ei[r * sl:(r + 1) * sl] + ci
        xr_rows.append(jnp.where(first, car_r, pltpu.roll(fr, 1, 0)))
        xi_rows.append(jnp.where(first, car_i, pltpu.roll(fi, 1, 0)))
        car_r, car_i = fr[sl - 1:sl], fi[sl - 1:sl]
    sr = jnp.concatenate(xr_rows, axis=0)
    si = jnp.concatenate(xi_rows, axis=0)

    def state_major(s):
        return jnp.concatenate([s[:, b * lw:(b + 1) * lw].T for b in range(nb)], axis=1)

    srt, sit = state_major(sr), state_major(si)
    for g in range(ng):
        s_in = jnp.concatenate([srt[g * p:(g + 1) * p], sit[g * p:(g + 1) * p]], axis=0).astype(BF16)
        y = ys[g] + _dot(o_ref[g], s_in) + dcol_ref[g] * ut_ref[g].astype(F32)
        yt_ref[:, g * h:(g + 1) * h, :] = y.reshape(th // h, h, rows).astype(BF16)


def _conv_branch_kernel(a_ref, halo_ref, am_ref, zc_ref, gc_ref, w_ref, b_ref, lng_ref, lnb_ref,
                        wc_ref, o_ref, s1_ref, s2_ref, yc_ref, cv_ref, *, width, ranges_per_seq):
    t, nr, e = a_ref.shape
    hr = halo_ref.shape[1]
    d = o_ref.shape[-1]
    seq_start = (pl.program_id(0) % ranges_per_seq) == 0
    for j in range(t):
        p1 = jnp.where(seq_start, am_ref[j:j + 1, :], halo_ref[j, hr - 1:hr, :])
        p2 = jnp.where(seq_start, 0.0, halo_ref[j, hr - 2:hr - 1, :])
        s1_ref[j, 0:1, :] = p1
        s1_ref[j, 1:nr, :] = a_ref[j, 0:nr - 1, :]
        s2_ref[j, 0:1, :] = p2
        s2_ref[j, 1:2, :] = p1
        s2_ref[j, 2:nr, :] = a_ref[j, 0:nr - 2, :]
    srcs = (a_ref, s1_ref, s2_ref)
    lng, lnb = lng_ref[...], lnb_ref[...]
    lw = LANES_V7X
    sl = w_ref.shape[1]
    for c0 in range(0, e, lw):
        wk = [w_ref[k, :, c0:c0 + lw] for k in range(width)]
        bias = jnp.broadcast_to(b_ref[:, c0:c0 + lw], (sl, lw))

        def row_tile(i, carry, c0=c0, wk=wk, bias=bias):
            r0 = pl.multiple_of(i * sl, sl)
            for jo in range(t):
                acc = bias
                for s in range(width):
                    r, q = s % t, s // t
                    src = srcs[q + (1 if r > jo else 0)]
                    acc = acc + wk[width - 1 - s] * src[(jo - r) % t, pl.ds(r0, sl), c0:c0 + lw]
                cv_ref[jo, pl.ds(r0, sl), c0:c0 + lw] = acc
            return carry

        lax.fori_loop(0, nr // sl, row_tile, 0)
    for jo in range(t):
        acc = cv_ref[jo]
        mu = jnp.mean(acc, axis=-1, keepdims=True)
        xc = acc - mu
        var = jnp.mean(xc * xc, axis=-1, keepdims=True)
        yn = xc * lax.rsqrt(var + LN_EPS) * lng + lnb
        yc_ref[jo * nr:(jo + 1) * nr, :] = (_silu(yn) * zc_ref[jo].astype(F32)).astype(BF16)
    pc = _dot(yc_ref[...], wc_ref[...])
    gate = gc_ref[...].reshape(t * nr, d).astype(F32)
    o_ref[...] = (gate * pc).astype(BF16).reshape(t, nr, d)


def _ssm_out_kernel(yt_ref, zt_ref, mc_ref, gs_ref, wg_ref, bcol_ref, ws_ref, w_ref, gain_ref,
                    x_hbm, o_hbm, xbuf, obuf, sem_in, sem_out):
    j, r = pl.program_id(0), pl.program_id(1)
    nr = pl.num_programs(1)
    step, n_steps = j * nr + r, pl.num_programs(0) * nr
    _, hb, nc, d = xbuf.shape
    slot = step % 2

    def x_copy(jj, rr, sl):
        return pltpu.make_async_copy(x_hbm.at[pl.ds(rr * hb, hb), :, jj, :], xbuf.at[sl], sem_in.at[sl])

    def o_copy(jj, rr, sl):
        return pltpu.make_async_copy(obuf.at[sl], o_hbm.at[pl.ds(rr * hb, hb), :, jj, :], sem_out.at[sl])

    @pl.when(step == 0)
    def _():
        x_copy(0, 0, 0).start()

    x_copy(j, r, slot).wait()

    @pl.when(step + 1 < n_steps)
    def _():
        nxt = step + 1
        x_copy(nxt // nr, nxt % nr, 1 - slot).start()

    @pl.when(step >= 2)
    def _():
        o_copy(j, r, slot).wait()

    y = jax.nn.gelu(yt_ref[...].astype(F32), approximate=True)
    g = _dot(wg_ref[...], y.astype(BF16)) + bcol_ref[...]
    y2t = (y * _sigmoid(g) * zt_ref[...].astype(F32)).astype(BF16)
    ps = _dot_tn(y2t, ws_ref[...])
    merged = (mc_ref[...].astype(F32) + gs_ref[...].astype(F32) * ps).astype(BF16)
    hres = xbuf[slot].reshape(hb * nc, d) + _dot(merged, w_ref[...])
    ms = jnp.mean(hres * hres, axis=-1, keepdims=True)
    obuf[slot] = (hres * lax.rsqrt(ms + NORM_EPS) * gain_ref[...]).reshape(hb, nc, d)
    o_copy(j, r, slot).start()

    @pl.when(step == n_steps - 1)
    def _():
        o_copy(j, r, 1 - slot).wait()
        o_copy(j, r, slot).wait()


def _col_tile(n):
    return 512 if n % 512 == 0 else n


def kernel(x, meta, norm_g, w_in, b_gate, dw_w, dw_b, ln_g, ln_b, w_conv, lam_re, lam_im,
           log_step, b_re, b_im, c_re, c_im, d_skip, w_glu, b_glu, w_ssm, w_out, final_g):
    bsz, seq, d = x.shape
    t = T_CHUNK
    e = w_conv.shape[0]
    g_n, p_n = lam_re.shape
    h_n = e // g_n
    width = dw_w.shape[0]
    nc = seq // t
    rows = bsz * nc
    assert meta.shape[0] == t and h_n * t == 2 * LANES_V7X and nc == LANES_V7X
    assert width <= 2 * t - 1 and seq % t == 0 and 2 * p_n == LANES_V7X and g_n % 2 == 0
    tn = _col_tile(e)
    ne = e // tn

    x4 = x.reshape(bsz, nc, t, d)
    gain = norm_g.reshape(1, d)

    assert 2 * d == N_GATE_BLOCKS * e and w_in.shape[1] == PROJ_STEPS * e and PROJ_STEPS % N_WSLOTS == 0
    hbm = pl.BlockSpec(memory_space=pl.ANY)
    a_slab, zc_slab, gates, ut, zst, w_in_b = pl.pallas_call(
        _proj_kernel,
        grid=(t,),
        in_specs=[hbm, hbm,
                  pl.BlockSpec((1, d), lambda j: (0, 0)),
                  pl.BlockSpec((1, 2 * d), lambda j: (0, 0))],
        out_specs=[hbm] * 6,
        out_shape=[jax.ShapeDtypeStruct((t, rows, e), F32),
                   jax.ShapeDtypeStruct((t, rows, e), BF16),
                   jax.ShapeDtypeStruct((t, rows, 2 * d), BF16),
                   jax.ShapeDtypeStruct((g_n, t * h_n, rows), BF16),
                   jax.ShapeDtypeStruct((t, e, rows), BF16),
                   jax.ShapeDtypeStruct(w_in.shape, BF16)],
        scratch_shapes=[pltpu.VMEM((bsz, nc, d), F32), pltpu.VMEM((rows, d), BF16),
                        pltpu.VMEM((N_WSLOTS, d, e), BF16), pltpu.VMEM((d, e), F32),
                        pltpu.VMEM((rows, e), F32), pltpu.VMEM((rows, e), BF16),
                        pltpu.VMEM((2, rows, e), BF16), pltpu.VMEM((g_n, h_n, rows), BF16),
                        pltpu.VMEM((e, rows), BF16),
                        pltpu.SemaphoreType.DMA((1,)), pltpu.SemaphoreType.DMA((N_WSLOTS,)),
                        pltpu.SemaphoreType.DMA((1,)), pltpu.SemaphoreType.DMA((N_WSLOTS,)),
                        pltpu.SemaphoreType.DMA((6,))],
        compiler_params=_cparams("arbitrary"),
        name="proj",
    )(x4, w_in, gain, b_gate.reshape(1, 2 * d))

    a_meta, u_meta = pl.pallas_call(
        _proj_meta_kernel,
        grid=(ne,),
        in_specs=[pl.BlockSpec((t, d), lambda n: (0, 0)), pl.BlockSpec((1, d), lambda n: (0, 0)),
                  pl.BlockSpec((d, tn), lambda n: (0, n)), pl.BlockSpec((d, tn), lambda n: (0, ne + n)),
                  pl.BlockSpec((d, tn), lambda n: (0, 3 * ne + n))],
        out_specs=[pl.BlockSpec((t, tn), lambda n: (0, n)), pl.BlockSpec((t, tn), lambda n: (0, n))],
        out_shape=[jax.ShapeDtypeStruct((t, e), F32), jax.ShapeDtypeStruct((t, e), F32)],
        compiler_params=_cparams("arbitrary"),
        name="proj_meta",
    )(meta, gain, w_in_b, w_in_b, w_in_b)
    um_col = u_meta.reshape(t, g_n, h_n).transpose(1, 0, 2).reshape(g_n, t * h_n, 1)

    grp = lambda *shape: pl.BlockSpec((None,) + shape, lambda g: (g,) + (0,) * len(shape))
    tile2 = lambda v: jnp.concatenate([v, v], axis=-1)
    gpp = 8
    gblk = lambda *shape: pl.BlockSpec((gpp,) + shape, lambda g: (g,) + (0,) * len(shape))
    m_op, r_op, o_op, a_pow = pl.pallas_call(
        _s5_param_kernel,
        grid=(g_n // gpp,),
        in_specs=[gblk(p_n, 1), gblk(p_n, 1), gblk(1, 2 * p_n), gblk(1, 2 * p_n), gblk(1, 1),
                  gblk(p_n, h_n), gblk(p_n, h_n), gblk(h_n, 2 * p_n), gblk(h_n, 2 * p_n)],
        out_specs=[gblk(t * h_n, t * h_n), gblk(2 * p_n, t * h_n), gblk(t * h_n, 2 * p_n),
                   gblk(2 * SUBLANES_V7X, 2 * p_n)],
        out_shape=[jax.ShapeDtypeStruct((g_n, t * h_n, t * h_n), BF16),
                   jax.ShapeDtypeStruct((g_n, 2 * p_n, t * h_n), BF16),
                   jax.ShapeDtypeStruct((g_n, t * h_n, 2 * p_n), BF16),
                   jax.ShapeDtypeStruct((g_n, 2 * SUBLANES_V7X, 2 * p_n), F32)],
        compiler_params=_cparams("parallel"),
        name="s5_params",
    )(lam_re.reshape(g_n, p_n, 1), lam_im.reshape(g_n, p_n, 1),
      tile2(lam_re).reshape(g_n, 1, 2 * p_n), tile2(lam_im).reshape(g_n, 1, 2 * p_n),
      log_step.reshape(g_n, 1, 1),
      b_re, b_im,
      jnp.concatenate([c_re, c_im], axis=-1), jnp.concatenate([c_im, c_re], axis=-1))
    pw_pair = (a_pow[:, :, :p_n].reshape(g_n // 2, 2, 2 * SUBLANES_V7X, p_n)
               .transpose(0, 2, 1, 3).reshape(g_n // 2, 2 * SUBLANES_V7X, 2 * p_n))

    dcol = jnp.tile(d_skip.reshape(g_n, 1, h_n), (1, t, 1)).reshape(g_n, t * h_n, 1)
    pair = lambda *shape: pl.BlockSpec((2,) + shape, lambda g: (g,) + (0,) * len(shape))
    yt = pl.pallas_call(
        _s5_scan_kernel,
        grid=(g_n // 2,),
        in_specs=[pair(t * h_n, rows), pair(t * h_n, 1), pair(t * h_n, t * h_n), pair(2 * p_n, t * h_n),
                  pair(t * h_n, 2 * p_n), grp(2 * SUBLANES_V7X, 2 * p_n), pair(t * h_n, 1)],
        out_specs=pl.BlockSpec((t, 2 * h_n, rows), lambda g: (0, g, 0)),
        out_shape=jax.ShapeDtypeStruct((t, e, rows), BF16),
        compiler_params=_cparams("parallel"),
        name="s5_scan",
    )(ut, um_col, m_op, r_op, o_op, pw_pair, dcol)

    nr = 64
    hr = SUBLANES_V7X
    ranges_per_seq = nc // nr
    mc = pl.pallas_call(
        functools.partial(_conv_branch_kernel, width=width, ranges_per_seq=ranges_per_seq),
        grid=(rows // nr,),
        in_specs=[pl.BlockSpec((t, nr, e), lambda q: (0, q, 0)),
                  pl.BlockSpec((t, hr, e), lambda q: (0, jnp.maximum(q * (nr // hr) - 1, 0), 0)),
                  pl.BlockSpec((t, e), lambda q: (0, 0)),
                  pl.BlockSpec((t, nr, e), lambda q: (0, q, 0)),
                  pl.BlockSpec((t, nr, d), lambda q: (0, q, 0)),
                  pl.BlockSpec((width, SUBLANES_V7X, e), lambda q: (0, 0, 0)),
                  pl.BlockSpec((1, e), lambda q: (0, 0)),
                  pl.BlockSpec((1, e), lambda q: (0, 0)),
                  pl.BlockSpec((1, e), lambda q: (0, 0)),
                  pl.BlockSpec((e, d), lambda q: (0, 0))],
        out_specs=pl.BlockSpec((t, nr, d), lambda q: (0, q, 0)),
        out_shape=jax.ShapeDtypeStruct((t, rows, d), BF16),
        scratch_shapes=[pltpu.VMEM((t, nr, e), F32), pltpu.VMEM((t, nr, e), F32),
                        pltpu.VMEM((t * nr, e), BF16), pltpu.VMEM((t, nr, e), F32)],
        compiler_params=_cparams("parallel"),
        name="conv_branch",
    )(a_slab, a_slab, a_meta, zc_slab, gates,
      jnp.broadcast_to(dw_w.reshape(width, 1, e), (width, SUBLANES_V7X, e)), dw_b.reshape(1, e),
      ln_g.reshape(1, e), ln_b.reshape(1, e), w_conv.astype(BF16))

    rt = rows // 2
    hb = bsz // 2
    const = lambda *shape: pl.BlockSpec(shape, lambda j, r: (0,) * len(shape), pipeline_mode=pl.Buffered(1))
    out = pl.pallas_call(
        _ssm_out_kernel,
        grid=(t, 2),
        in_specs=[pl.BlockSpec((None, e, rt), lambda j, r: (j, 0, r)),
                  pl.BlockSpec((None, e, rt), lambda j, r: (j, 0, r)),
                  pl.BlockSpec((None, rt, d), lambda j, r: (j, r, 0)),
                  pl.BlockSpec((None, rt, d), lambda j, r: (j, r, 1)),
                  const(e, e), const(e, 1), const(e, d), const(d, d), const(1, d),
                  pl.BlockSpec(memory_space=pl.ANY)],
        out_specs=pl.BlockSpec(memory_space=pl.ANY),
        out_shape=jax.ShapeDtypeStruct((bsz, nc, t, d), F32),
        scratch_shapes=[pltpu.VMEM((2, hb, nc, d), F32), pltpu.VMEM((2, hb, nc, d), F32),
                        pltpu.SemaphoreType.DMA((2,)), pltpu.SemaphoreType.DMA((2,))],
        compiler_params=_cparams("arbitrary", "arbitrary"),
        name="ssm_out",
    )(yt, zst, mc, gates, w_glu.T.astype(BF16), b_glu.reshape(e, 1), w_ssm.astype(BF16),
      w_out.astype(BF16), final_g.reshape(1, d), x4)
    return out.reshape(bsz, seq, d)
```

```python
import functools
import math

import jax
import jax.numpy as jnp
from jax import lax
from jax.experimental import pallas as pl
from jax.experimental.pallas import tpu as pltpu

F32 = jnp.float32
BF16 = jnp.bfloat16

T_CHUNK = 16
LANES_V7X = 128
SUBLANES_V7X = 8
NORM_EPS = 1e-6
LN_EPS = 1e-5
LAM_RE_MAX = -1e-4
VMEM_LIMIT_V7X = 56 * 1024 * 1024


def _cparams(*sem):
    return pltpu.CompilerParams(dimension_semantics=sem, vmem_limit_bytes=VMEM_LIMIT_V7X)


def _sigmoid(x):
    return 0.5 * jnp.tanh(0.5 * x) + 0.5


def _silu(x):
    return x * _sigmoid(x)


def _dot(a, b):
    return jnp.dot(a, b, preferred_element_type=F32)


def _dot_nt(a, b):
    return lax.dot_general(a, b, (((1,), (1,)), ((), ())), preferred_element_type=F32)


def _dot_tn(a, b):
    return lax.dot_general(a, b, (((0,), (0,)), ((), ())), preferred_element_type=F32)


def _dot_tt(a, b):
    return lax.dot_general(a, b, (((0,), (1,)), ((), ())), preferred_element_type=F32)


def _cmul(ar, ai, br, bi):
    return ar * br - ai * bi, ar * bi + ai * br


def _norm_slab(x_ref, gain_ref, xn_ref):
    nb, nc, _ = x_ref.shape
    gain = gain_ref[...]

    def body(b, carry):
        x = x_ref[b]
        ms = jnp.mean(x * x, axis=-1, keepdims=True)
        r0 = pl.multiple_of(b * nc, nc)
        xn_ref[pl.ds(r0, nc), :] = (x * lax.rsqrt(ms + NORM_EPS) * gain).astype(BF16)
        return carry

    lax.fori_loop(0, nb, body, 0)


PROJ_STEPS = 9
N_WSLOTS = 3
PROJ_SECTION = (1, 0, 2, 5, 6, 7, 8, 3, 4)
STEP_G, STEP_V, STEP_ZC, STEP_GATE0, STEP_U, STEP_ZS = 0, 1, 2, 3, 7, 8
N_GATE_BLOCKS = STEP_U - STEP_GATE0


def _proj_kernel(x_hbm, wf_hbm, gain_ref, bg_ref, a_hbm, zc_hbm, gt_hbm, ut_hbm, zt_hbm, wb_hbm,
                 xbuf, xn_ref, wbuf, wf32, a_st, zc_st, gt_st, ut_st, zt_st,
                 sem_x, sem_w, sem_wf, sem_ws, sem_o):
    j, nj = pl.program_id(0), pl.num_programs(0)
    e = wbuf.shape[2]
    h = ut_st.shape[1]
    not_first = j > 0

    def x_copy(jj):
        return pltpu.make_async_copy(x_hbm.at[:, :, jj, :], xbuf, sem_x.at[0])

    def section(ref, k):
        return ref.at[:, pl.ds(PROJ_SECTION[k] * e, e)]

    def w_copy(k, slot):
        return pltpu.make_async_copy(section(wb_hbm, k), wbuf.at[slot], sem_w.at[slot])

    def wf_copy(k):
        return pltpu.make_async_copy(section(wf_hbm, k), wf32, sem_wf.at[0])

    def ws_copy(k, slot):
        return pltpu.make_async_copy(wbuf.at[slot], section(wb_hbm, k), sem_ws.at[slot])

    def a_copy():
        return pltpu.make_async_copy(a_st, a_hbm.at[j], sem_o.at[0])

    def zc_copy():
        return pltpu.make_async_copy(zc_st, zc_hbm.at[j], sem_o.at[1])

    def gt_copy(g):
        return pltpu.make_async_copy(gt_st.at[g % 2], gt_hbm.at[j, :, pl.ds(g * e, e)], sem_o.at[2 + g % 2])

    def ut_copy():
        return pltpu.make_async_copy(ut_st, ut_hbm.at[:, pl.ds(j * h, h), :], sem_o.at[4])

    def zt_copy():
        return pltpu.make_async_copy(zt_st, zt_hbm.at[j], sem_o.at[5])

    def wait_previous(copy):
        @pl.when(not_first)
        def _():
            copy.wait()

    @pl.when(j == 0)
    def _():
        x_copy(0).start()
        wf_copy(0).start()

    x_copy(j).wait()
    _norm_slab(xbuf, gain_ref, xn_ref)

    @pl.when(j + 1 < nj)
    def _():
        x_copy(j + 1).start()

    xn = xn_ref

    def cast_section(slot):
        rb = 256

        def body(i, carry):
            r0 = pl.multiple_of(i * rb, rb)
            wbuf[slot, pl.ds(r0, rb), :] = wf32[pl.ds(r0, rb), :].astype(BF16)
            return carry

        lax.fori_loop(0, wf32.shape[0] // rb, body, 0)

    def acquire(k):
        slot = k % N_WSLOTS

        @pl.when(j == 0)
        def _():
            wf_copy(k).wait()
            if k >= N_WSLOTS:
                ws_copy(k - N_WSLOTS, slot).wait()
            cast_section(slot)
            if k + 1 < PROJ_STEPS:
                wf_copy(k + 1).start()
            ws_copy(k, slot).start()

        @pl.when(not_first)
        def _():
            w_copy(k, slot).wait()
            ahead = k + N_WSLOTS - 1
            if ahead < PROJ_STEPS:
                w_copy(ahead, ahead % N_WSLOTS).start()
            else:
                @pl.when(j + 1 < nj)
                def _():
                    w_copy(ahead - PROJ_STEPS, ahead % N_WSLOTS).start()

    def free_staging(k):
        if k == STEP_G:
            wait_previous(a_copy())
        elif k == STEP_ZC:
            wait_previous(zc_copy())
        elif STEP_GATE0 <= k < STEP_U:
            g = k - STEP_GATE0
            if g < 2:
                wait_previous(gt_copy(g))
            else:
                gt_copy(g).wait()
        elif k == STEP_U:
            wait_previous(ut_copy())
        elif k == STEP_ZS:
            wait_previous(zt_copy())

    def finish(k, acc):
        if k == STEP_G:
            a_st[...] = _sigmoid(acc)
        elif k == STEP_V:
            a_st[...] = a_st[...] * acc
            a_copy().start()
        elif k == STEP_ZC:
            zc_st[...] = _silu(acc).astype(BF16)
            zc_copy().start()
        elif k < STEP_U:
            g = k - STEP_GATE0
            gt_st[g % 2] = _sigmoid(acc + bg_ref[:, g * e:(g + 1) * e]).astype(BF16)
            gt_copy(g).start()
        elif k == STEP_U:
            ut_st[...] = acc.reshape(ut_st.shape).astype(BF16)
            ut_copy().start()
        else:
            zt_st[...] = _silu(acc).astype(BF16)
            zt_copy().start()

    for k in range(PROJ_STEPS):
        acquire(k)
        free_staging(k)
        w = wbuf[k % N_WSLOTS]
        finish(k, _dot_tt(w, xn[...]) if k >= STEP_U else _dot(xn[...], w))

    @pl.when((j == 0) & (nj > 1))
    def _():
        for k in range(PROJ_STEPS - N_WSLOTS, PROJ_STEPS):
            ws_copy(k, k % N_WSLOTS).wait()
        for k in range(N_WSLOTS - 1):
            w_copy(k, k).start()

    @pl.when((j == 0) & (nj == 1))
    def _():
        for k in range(PROJ_STEPS - N_WSLOTS, PROJ_STEPS):
            ws_copy(k, k % N_WSLOTS).wait()

    @pl.when(j == nj - 1)
    def _():
        a_copy().wait()
        zc_copy().wait()
        gt_copy(N_GATE_BLOCKS - 2).wait()
        gt_copy(N_GATE_BLOCKS - 1).wait()
        ut_copy().wait()
        zt_copy().wait()


def _proj_meta_kernel(m_ref, gain_ref, wv_ref, wg_ref, wu_ref, a_ref, u_ref):
    x = m_ref[...]
    ms = jnp.mean(x * x, axis=-1, keepdims=True)
    xn = (x * lax.rsqrt(ms + NORM_EPS) * gain_ref[...]).astype(BF16)
    a_ref[...] = _dot(xn, wv_ref[...]) * _sigmoid(_dot(xn, wg_ref[...]))
    u_ref[...] = _dot(xn, wu_ref[...])


def _lam_bar(lr, li, dt):
    lr = jnp.minimum(lr, LAM_RE_MAX)
    mag = jnp.exp(lr * dt)
    th = li * dt
    return lr, mag * jnp.cos(th), mag * jnp.sin(th)


def _s5_param_kernel(*refs):
    for g in range(refs[0].shape[0]):
        _s5_param_group(*[ref.at[g] for ref in refs])


def _s5_param_group(lrc_ref, lic_ref, lrr_ref, lir_ref, ls_ref, btr_ref, bti_ref,
                    ca_ref, cb_ref, m_ref, r_ref, o_ref, pw_ref):
    p = lrc_ref.shape[0]
    t = T_CHUNK
    h = btr_ref.shape[1]
    dt = jnp.exp(ls_ref[...])

    sel = (lax.broadcasted_iota(jnp.int32, (h, t * h), 1) % h
           == lax.broadcasted_iota(jnp.int32, (h, t * h), 0)).astype(F32)
    tile_b = lambda ref: jnp.dot(ref[...], sel, preferred_element_type=F32, precision=lax.Precision.HIGHEST)

    _, l1r, l1i = _lam_bar(lrr_ref[...], lir_ref[...], dt)
    eye = (lax.broadcasted_iota(jnp.int32, (p, 2 * p), 0) == lax.broadcasted_iota(jnp.int32, (p, 2 * p), 1))
    to_col = lambda row: jnp.sum(jnp.where(eye, row, 0.0), axis=1, keepdims=True)
    lam_r, lam_i = to_col(l1r), to_col(l1i)
    lr = jnp.minimum(lrc_ref[...], LAM_RE_MAX)
    li = lic_ref[...]
    den = lr * lr + li * li
    nr, ni = lam_r - 1.0, lam_i
    kr = (nr * lr + ni * li) / den
    ki = (ni * lr - nr * li) / den
    bbr, bbi = _cmul(kr, ki, tile_b(btr_ref), tile_b(bti_ref))

    lane = lax.broadcasted_iota(jnp.int32, (p, t * h), 1)
    expo = (t - 1) - lane // h
    pr = jnp.ones((p, t * h), F32)
    pi = jnp.zeros((p, t * h), F32)
    br_, bi_ = lam_r, lam_i
    for k in range(int(math.log2(t))):
        qr, qi = _cmul(pr, pi, br_, bi_)
        bit = ((expo >> k) & 1) == 1
        pr = jnp.where(bit, qr, pr)
        pi = jnp.where(bit, qi, pi)
        br_, bi_ = _cmul(br_, bi_, br_, bi_)
    wr, wi = _cmul(pr, pi, bbr, bbi)
    r_ref[0:p, :] = wr.astype(BF16)
    r_ref[p:2 * p, :] = wi.astype(BF16)

    ca = ca_ref[...]
    cb = cb_ref[...]
    sgn = jnp.where(lax.broadcasted_iota(jnp.int32, ca.shape, 1) < p, 1.0, -1.0)
    w = jnp.concatenate([wr, wi], axis=0)
    krev = jnp.dot(sgn * ca, w, preferred_element_type=F32, precision=lax.Precision.HIGHEST)

    half = LANES_V7X
    ka, kb = krev[:, :half], krev[:, half:]
    lane_h = lax.broadcasted_iota(jnp.int32, (h, half), 1)
    zero = jnp.zeros((h, half), F32)
    for i in range(t):
        s = (t - 1 - i) * h
        if s == 0:
            lo, hi = ka, kb
        elif s < half:
            ra = pltpu.roll(ka, half - s, 1)
            rb = pltpu.roll(kb, half - s, 1)
            keep = lane_h < (half - s)
            lo, hi = jnp.where(keep, ra, rb), jnp.where(keep, rb, zero)
        elif s == half:
            lo, hi = kb, zero
        else:
            rb = pltpu.roll(kb, 2 * half - s, 1)
            lo, hi = jnp.where(lane_h < (2 * half - s), rb, zero), zero
        m_ref[i * h:(i + 1) * h, 0:half] = lo.astype(BF16)
        m_ref[i * h:(i + 1) * h, half:2 * half] = hi.astype(BF16)

    qr, qi = l1r, l1i
    for i in range(t):
        o_ref[i * h:(i + 1) * h, :] = (sgn * (ca * qr) - cb * qi).astype(BF16)
        qr, qi = _cmul(qr, qi, l1r, l1i)

    ar, ai = l1r, l1i
    for _ in range(int(math.log2(t))):
        ar, ai = _cmul(ar, ai, ar, ai)
    qr, qi = ar, ai
    for s in range(SUBLANES_V7X):
        pw_ref[s:s + 1, :] = qr
        pw_ref[SUBLANES_V7X + s:SUBLANES_V7X + s + 1, :] = qi
        qr, qi = _cmul(qr, qi, ar, ai)


def _s5_scan_kernel(ut_ref, um_ref, m_ref, r_ref, o_ref, pw_ref, dcol_ref, yt_ref):
    ng, th, rows = ut_ref.shape
    p = r_ref.shape[1] // 2
    lw, sl = LANES_V7X, SUBLANES_V7X
    nb = rows // lw
    h = yt_ref.shape[1] // ng
    lane0 = lax.broadcasted_iota(jnp.int32, (th, lw), 1) == 0

    ys, zre, zim = [], [], []
    for g in range(ng):
        meta_blk = jnp.where(lane0, um_ref[g], 0.0).astype(BF16)
        ue = jnp.concatenate([ut_ref[g], meta_blk], axis=1)
        ys.append(_dot(m_ref[g], ut_ref[g]))
        z = _dot(r_ref[g], ue)
        zre.append(z[:p])
        zim.append(z[p:])
    zr = jnp.concatenate(zre, axis=0)
    zi = jnp.concatenate(zim, axis=0)

    def chunk_major(z):
        return jnp.concatenate([z[:, b * lw:(b + 1) * lw].T for b in range(nb)], axis=1)

    def tile_lanes(v):
        return jnp.concatenate([v] * nb, axis=1)

    er, ei = chunk_major(zr), chunk_major(zi)
    car_r = tile_lanes(zr[:, rows:rows + lw].T[0:1, :])
    car_i = tile_lanes(zi[:, rows:rows + lw].T[0:1, :])
    pw = tile_lanes(pw_ref[...])
    nc, width = er.shape
    srow = lax.broadcasted_iota(jnp.int32, (nc, width), 0) % sl

    def tile_roll(x, sh):
        return pltpu.roll(x.reshape(nc // sl, sl, width), sh, 1).reshape(nc, width)

    for sh in (1, 2, 4):
        dr, di = _cmul(pw[sh - 1:sh], pw[sl + sh - 1:sl + sh], tile_roll(er, sh), tile_roll(ei, sh))
        ok = srow >= sh
        er = er + jnp.where(ok, dr, 0.0)
        ei = ei + jnp.where(ok, di, 0.0)

    a8r, a8i = pw[0:sl], pw[sl:2 * sl]
    first = lax.broadcasted_iota(jnp.int32, (sl, width), 0) == 0
    xr_rows, xi_rows = [], []
    for r in range(nc // sl):
        cr, ci = _cmul(a8r, a8i, car_r, car_i)
        fr = er[r * sl:(r + 1) * sl] + cr
        fi = ei[r * sl:(r + 1) * sl] + ci
        xr_rows.append(jnp.where(first, car_r, pltpu.roll(fr, 1, 0)))
        xi_rows.append(jnp.where(first, car_i, pltpu.roll(fi, 1, 0)))
        car_r, car_i = fr[sl - 1:sl], fi[sl - 1:sl]
    sr = jnp.concatenate(xr_rows, axis=0)
    si = jnp.concatenate(xi_rows, axis=0)

    def state_major(s):
        return jnp.concatenate([s[:, b * lw:(b + 1) * lw].T for b in range(nb)], axis=1)

    srt, sit = state_major(sr), state_major(si)
    for g in range(ng):
        s_in = jnp.concatenate([srt[g * p:(g + 1) * p], sit[g * p:(g + 1) * p]], axis=0).astype(BF16)
        y = ys[g] + _dot(o_ref[g], s_in) + dcol_ref[g] * ut_ref[g].astype(F32)
        yt_ref[:, g * h:(g + 1) * h, :] = y.reshape(th // h, h, rows).astype(BF16)


def _conv_branch_kernel(a_ref, halo_ref, am_ref, zc_ref, gc_ref, w_ref, b_ref, lng_ref, lnb_ref,
                        wc_ref, o_ref, s1_ref, s2_ref, yc_ref, cv_ref, *, width, ranges_per_seq):
    t, nr, e = a_ref.shape
    hr = halo_ref.shape[1]
    d = o_ref.shape[-1]
    seq_start = (pl.program_id(0) % ranges_per_seq) == 0
    for j in range(t):
        p1 = jnp.where(seq_start, am_ref[j:j + 1, :], halo_ref[j, hr - 1:hr, :])
        p2 = jnp.where(seq_start, 0.0, halo_ref[j, hr - 2:hr - 1, :])
        s1_ref[j, 0:1, :] = p1
        s1_ref[j, 1:nr, :] = a_ref[j, 0:nr - 1, :]
        s2_ref[j, 0:1, :] = p2
        s2_ref[j, 1:2, :] = p1
        s2_ref[j, 2:nr, :] = a_ref[j, 0:nr - 2, :]
    srcs = (a_ref, s1_ref, s2_ref)
    lng, lnb = lng_ref[...], lnb_ref[...]
    lw = LANES_V7X
    sl = w_ref.shape[1]
    for c0 in range(0, e, lw):
        wk = [w_ref[k, :, c0:c0 + lw] for k in range(width)]
        bias = jnp.broadcast_to(b_ref[:, c0:c0 + lw], (sl, lw))

        def row_tile(i, carry, c0=c0, wk=wk, bias=bias):
            r0 = pl.multiple_of(i * sl, sl)
            for jo in range(t):
                acc = bias
                for s in range(width):
                    r, q = s % t, s // t
                    src = srcs[q + (1 if r > jo else 0)]
                    acc = acc + wk[width - 1 - s] * src[(jo - r) % t, pl.ds(r0, sl), c0:c0 + lw]
                cv_ref[jo, pl.ds(r0, sl), c0:c0 + lw] = acc
            return carry

        lax.fori_loop(0, nr // sl, row_tile, 0)
    n_parts = 4
    th = t // n_parts
    for part in range(n_parts):
        for jo in range(part * th, (part + 1) * th):
            acc = cv_ref[jo]
            mu = jnp.mean(acc, axis=-1, keepdims=True)
            xc = acc - mu
            var = jnp.mean(xc * xc, axis=-1, keepdims=True)
            yn = xc * lax.rsqrt(var + LN_EPS) * lng + lnb
            yc_ref[jo * nr:(jo + 1) * nr, :] = (_silu(yn) * zc_ref[jo].astype(F32)).astype(BF16)
        rows = slice(part * th * nr, (part + 1) * th * nr)
        slabs = slice(part * th, (part + 1) * th)
        pc = _dot(yc_ref[rows, :], wc_ref[...])
        gate = gc_ref[slabs].reshape(th * nr, d).astype(F32)
        o_ref[slabs] = (gate * pc).astype(BF16).reshape(th, nr, d)


def _ssm_out_kernel(yt_ref, zt_ref, mc_ref, gs_ref, wg_ref, bcol_ref, ws_ref, w_ref, gain_ref,
                    x_hbm, o_hbm, xbuf, obuf, sem_in, sem_out):
    j, r = pl.program_id(0), pl.program_id(1)
    nr = pl.num_programs(1)
    step, n_steps = j * nr + r, pl.num_programs(0) * nr
    _, hb, nc, d = xbuf.shape
    slot = step % 2

    def x_copy(jj, rr, sl):
        return pltpu.make_async_copy(x_hbm.at[pl.ds(rr * hb, hb), :, jj, :], xbuf.at[sl], sem_in.at[sl])

    def o_copy(jj, rr, sl):
        return pltpu.make_async_copy(obuf.at[sl], o_hbm.at[pl.ds(rr * hb, hb), :, jj, :], sem_out.at[sl])

    @pl.when(step == 0)
    def _():
        x_copy(0, 0, 0).start()

    x_copy(j, r, slot).wait()

    @pl.when(step + 1 < n_steps)
    def _():
        nxt = step + 1
        x_copy(nxt // nr, nxt % nr, 1 - slot).start()

    @pl.when(step >= 2)
    def _():
        o_copy(j, r, slot).wait()

    y = jax.nn.gelu(yt_ref[...].astype(F32), approximate=True)
    g = _dot(wg_ref[...], y.astype(BF16)) + bcol_ref[...]
    y2t = (y * _sigmoid(g) * zt_ref[...].astype(F32)).astype(BF16)
    ps = _dot_tn(y2t, ws_ref[...])
    merged = (mc_ref[...].astype(F32) + gs_ref[...].astype(F32) * ps).astype(BF16)
    hres = xbuf[slot].reshape(hb * nc, d) + _dot(merged, w_ref[...])
    ms = jnp.mean(hres * hres, axis=-1, keepdims=True)
    obuf[slot] = (hres * lax.rsqrt(ms + NORM_EPS) * gain_ref[...]).reshape(hb, nc, d)
    o_copy(j, r, slot).start()

    @pl.when(step == n_steps - 1)
    def _():
        o_copy(j, r, 1 - slot).wait()
        o_copy(j, r, slot).wait()


def _col_tile(n):
    return 512 if n % 512 == 0 else n


def kernel(x, meta, norm_g, w_in, b_gate, dw_w, dw_b, ln_g, ln_b, w_conv, lam_re, lam_im,
           log_step, b_re, b_im, c_re, c_im, d_skip, w_glu, b_glu, w_ssm, w_out, final_g):
    bsz, seq, d = x.shape
    t = T_CHUNK
    e = w_conv.shape[0]
    g_n, p_n = lam_re.shape
    h_n = e // g_n
    width = dw_w.shape[0]
    nc = seq // t
    rows = bsz * nc
    assert meta.shape[0] == t and h_n * t == 2 * LANES_V7X and nc == LANES_V7X
    assert width <= 2 * t - 1 and seq % t == 0 and 2 * p_n == LANES_V7X and g_n % 2 == 0
    tn = _col_tile(e)
    ne = e // tn

    x4 = x.reshape(bsz, nc, t, d)
    gain = norm_g.reshape(1, d)

    assert 2 * d == N_GATE_BLOCKS * e and w_in.shape[1] == PROJ_STEPS * e and PROJ_STEPS % N_WSLOTS == 0
    hbm = pl.BlockSpec(memory_space=pl.ANY)
    a_slab, zc_slab, gates, ut, zst, w_in_b = pl.pallas_call(
        _proj_kernel,
        grid=(t,),
        in_specs=[hbm, hbm,
                  pl.BlockSpec((1, d), lambda j: (0, 0)),
                  pl.BlockSpec((1, 2 * d), lambda j: (0, 0))],
        out_specs=[hbm] * 6,
        out_shape=[jax.ShapeDtypeStruct((t, rows, e), F32),
                   jax.ShapeDtypeStruct((t, rows, e), BF16),
                   jax.ShapeDtypeStruct((t, rows, 2 * d), BF16),
                   jax.ShapeDtypeStruct((g_n, t * h_n, rows), BF16),
                   jax.ShapeDtypeStruct((t, e, rows), BF16),
                   jax.ShapeDtypeStruct(w_in.shape, BF16)],
        scratch_shapes=[pltpu.VMEM((bsz, nc, d), F32), pltpu.VMEM((rows, d), BF16),
                        pltpu.VMEM((N_WSLOTS, d, e), BF16), pltpu.VMEM((d, e), F32),
                        pltpu.VMEM((rows, e), F32), pltpu.VMEM((rows, e), BF16),
                        pltpu.VMEM((2, rows, e), BF16), pltpu.VMEM((g_n, h_n, rows), BF16),
                        pltpu.VMEM((e, rows), BF16),
                        pltpu.SemaphoreType.DMA((1,)), pltpu.SemaphoreType.DMA((N_WSLOTS,)),
                        pltpu.SemaphoreType.DMA((1,)), pltpu.SemaphoreType.DMA((N_WSLOTS,)),
                        pltpu.SemaphoreType.DMA((6,))],
        compiler_params=_cparams("arbitrary"),
        name="proj",
    )(x4, w_in, gain, b_gate.reshape(1, 2 * d))

    a_meta, u_meta = pl.pallas_call(
        _proj_meta_kernel,
        grid=(ne,),
        in_specs=[pl.BlockSpec((t, d), lambda n: (0, 0)), pl.BlockSpec((1, d), lambda n: (0, 0)),
                  pl.BlockSpec((d, tn), lambda n: (0, n)), pl.BlockSpec((d, tn), lambda n: (0, ne + n)),
                  pl.BlockSpec((d, tn), lambda n: (0, 3 * ne + n))],
        out_specs=[pl.BlockSpec((t, tn), lambda n: (0, n)), pl.BlockSpec((t, tn), lambda n: (0, n))],
        out_shape=[jax.ShapeDtypeStruct((t, e), F32), jax.ShapeDtypeStruct((t, e), F32)],
        compiler_params=_cparams("arbitrary"),
        name="proj_meta",
    )(meta, gain, w_in_b, w_in_b, w_in_b)
    um_col = u_meta.reshape(t, g_n, h_n).transpose(1, 0, 2).reshape(g_n, t * h_n, 1)

    grp = lambda *shape: pl.BlockSpec((None,) + shape, lambda g: (g,) + (0,) * len(shape))
    tile2 = lambda v: jnp.concatenate([v, v], axis=-1)
    gpp = 8
    gblk = lambda *shape: pl.BlockSpec((gpp,) + shape, lambda g: (g,) + (0,) * len(shape))
    m_op, r_op, o_op, a_pow = pl.pallas_call(
        _s5_param_kernel,
        grid=(g_n // gpp,),
        in_specs=[gblk(p_n, 1), gblk(p_n, 1), gblk(1, 2 * p_n), gblk(1, 2 * p_n), gblk(1, 1),
                  gblk(p_n, h_n), gblk(p_n, h_n), gblk(h_n, 2 * p_n), gblk(h_n, 2 * p_n)],
        out_specs=[gblk(t * h_n, t * h_n), gblk(2 * p_n, t * h_n), gblk(t * h_n, 2 * p_n),
                   gblk(2 * SUBLANES_V7X, 2 * p_n)],
        out_shape=[jax.ShapeDtypeStruct((g_n, t * h_n, t * h_n), BF16),
                   jax.ShapeDtypeStruct((g_n, 2 * p_n, t * h_n), BF16),
                   jax.ShapeDtypeStruct((g_n, t * h_n, 2 * p_n), BF16),
                   jax.ShapeDtypeStruct((g_n, 2 * SUBLANES_V7X, 2 * p_n), F32)],
        compiler_params=_cparams("parallel"),
        name="s5_params",
    )(lam_re.reshape(g_n, p_n, 1), lam_im.reshape(g_n, p_n, 1),
      tile2(lam_re).reshape(g_n, 1, 2 * p_n), tile2(lam_im).reshape(g_n, 1, 2 * p_n),
      log_step.reshape(g_n, 1, 1),
      b_re, b_im,
      jnp.concatenate([c_re, c_im], axis=-1), jnp.concatenate([c_im, c_re], axis=-1))
    pw_pair = (a_pow[:, :, :p_n].reshape(g_n // 2, 2, 2 * SUBLANES_V7X, p_n)
               .transpose(0, 2, 1, 3).reshape(g_n // 2, 2 * SUBLANES_V7X, 2 * p_n))

    dcol = jnp.tile(d_skip.reshape(g_n, 1, h_n), (1, t, 1)).reshape(g_n, t * h_n, 1)
    pair = lambda *shape: pl.BlockSpec((2,) + shape, lambda g: (g,) + (0,) * len(shape))
    yt = pl.pallas_call(
        _s5_scan_kernel,
        grid=(g_n // 2,),
        in_specs=[pair(t * h_n, rows), pair(t * h_n, 1), pair(t * h_n, t * h_n), pair(2 * p_n, t * h_n),
                  pair(t * h_n, 2 * p_n), grp(2 * SUBLANES_V7X, 2 * p_n), pair(t * h_n, 1)],
        out_specs=pl.BlockSpec((t, 2 * h_n, rows), lambda g: (0, g, 0)),
        out_shape=jax.ShapeDtypeStruct((t, e, rows), BF16),
        compiler_params=_cparams("parallel"),
        name="s5_scan",
    )(ut, um_col, m_op, r_op, o_op, pw_pair, dcol)

    nr = 64
    hr = SUBLANES_V7X
    ranges_per_seq = nc // nr
    mc = pl.pallas_call(
        functools.partial(_conv_branch_kernel, width=width, ranges_per_seq=ranges_per_seq),
        grid=(rows // nr,),
        in_specs=[pl.BlockSpec((t, nr, e), lambda q: (0, q, 0)),
                  pl.BlockSpec((t, hr, e), lambda q: (0, jnp.maximum(q * (nr // hr) - 1, 0), 0)),
                  pl.BlockSpec((t, e), lambda q: (0, 0)),
                  pl.BlockSpec((t, nr, e), lambda q: (0, q, 0)),
                  pl.BlockSpec((t, nr, d), lambda q: (0, q, 0)),
                  pl.BlockSpec((width, SUBLANES_V7X, e), lambda q: (0, 0, 0)),
                  pl.BlockSpec((1, e), lambda q: (0, 0)),
                  pl.BlockSpec((1, e), lambda q: (0, 0)),
                  pl.BlockSpec((1, e), lambda q: (0, 0)),
                  pl.BlockSpec((e, d), lambda q: (0, 0))],
        out_specs=pl.BlockSpec((t, nr, d), lambda q: (0, q, 0)),
        out_shape=jax.ShapeDtypeStruct((t, rows, d), BF16),
        scratch_shapes=[pltpu.VMEM((t, nr, e), F32), pltpu.VMEM((t, nr, e), F32),
                        pltpu.VMEM((t * nr, e), BF16), pltpu.VMEM((t, nr, e), F32)],
        compiler_params=_cparams("parallel"),
        name="conv_branch",
    )(a_slab, a_slab, a_meta, zc_slab, gates,
      jnp.broadcast_to(dw_w.reshape(width, 1, e), (width, SUBLANES_V7X, e)), dw_b.reshape(1, e),
      ln_g.reshape(1, e), ln_b.reshape(1, e), w_conv.astype(BF16))

    rt = rows // 2
    hb = bsz // 2
    const = lambda *shape: pl.BlockSpec(shape, lambda j, r: (0,) * len(shape), pipeline_mode=pl.Buffered(1))
    out = pl.pallas_call(
        _ssm_out_kernel,
        grid=(t, 2),
        in_specs=[pl.BlockSpec((None, e, rt), lambda j, r: (j, 0, r)),
                  pl.BlockSpec((None, e, rt), lambda j, r: (j, 0, r)),
                  pl.BlockSpec((None, rt, d), lambda j, r: (j, r, 0)),
                  pl.BlockSpec((None, rt, d), lambda j, r: (j, r, 1)),
                  const(e, e), const(e, 1), const(e, d), const(d, d), const(1, d),
                  pl.BlockSpec(memory_space=pl.ANY)],
        out_specs=pl.BlockSpec(memory_space=pl.ANY),
        out_shape=jax.ShapeDtypeStruct((bsz, nc, t, d), F32),
        scratch_shapes=[pltpu.VMEM((2, hb, nc, d), F32), pltpu.VMEM((2, hb, nc, d), F32),
                        pltpu.SemaphoreType.DMA((2,)), pltpu.SemaphoreType.DMA((2,))],
        compiler_params=_cparams("arbitrary", "arbitrary"),
        name="ssm_out",
    )(yt, zst, mc, gates, w_glu.T.astype(BF16), b_glu.reshape(e, 1), w_ssm.astype(BF16),
      w_out.astype(BF16), final_g.reshape(1, d), x4)
    return out.reshape(bsz, seq, d)
```

```python
import functools
import math

import jax
import jax.numpy as jnp
from jax import lax
from jax.experimental import pallas as pl
from jax.experimental.pallas import tpu as pltpu

F32 = jnp.float32
BF16 = jnp.bfloat16

T_CHUNK = 16
LANES_V7X = 128
SUBLANES_V7X = 8
NORM_EPS = 1e-6
LN_EPS = 1e-5
LAM_RE_MAX = -1e-4
VMEM_LIMIT_V7X = 56 * 1024 * 1024


def _cparams(*sem):
    return pltpu.CompilerParams(dimension_semantics=sem, vmem_limit_bytes=VMEM_LIMIT_V7X)


def _sigmoid(x):
    return 0.5 * jnp.tanh(0.5 * x) + 0.5


def _silu(x):
    return x * _sigmoid(x)


def _dot(a, b):
    return jnp.dot(a, b, preferred_element_type=F32)


def _dot_nt(a, b):
    return lax.dot_general(a, b, (((1,), (1,)), ((), ())), preferred_element_type=F32)


def _dot_tn(a, b):
    return lax.dot_general(a, b, (((0,), (0,)), ((), ())), preferred_element_type=F32)


def _dot_tt(a, b):
    return lax.dot_general(a, b, (((0,), (1,)), ((), ())), preferred_element_type=F32)


def _cmul(ar, ai, br, bi):
    return ar * br - ai * bi, ar * bi + ai * br


def _norm_slab(x_ref, gain_ref, xn_ref):
    nb, nc, _ = x_ref.shape
    gain = gain_ref[...]

    def body(b, carry):
        x = x_ref[b]
        ms = jnp.mean(x * x, axis=-1, keepdims=True)
        r0 = pl.multiple_of(b * nc, nc)
        xn_ref[pl.ds(r0, nc), :] = (x * lax.rsqrt(ms + NORM_EPS) * gain).astype(BF16)
        return carry

    lax.fori_loop(0, nb, body, 0)


PROJ_STEPS = 9
N_WSLOTS = 3
PROJ_SECTION = (1, 0, 2, 5, 6, 7, 8, 3, 4)
STEP_G, STEP_V, STEP_ZC, STEP_GATE0, STEP_U, STEP_ZS = 0, 1, 2, 3, 7, 8
N_GATE_BLOCKS = STEP_U - STEP_GATE0


def _proj_kernel(x_hbm, wf_hbm, gain_ref, bg_ref, a_hbm, zc_hbm, gt_hbm, ut_hbm, zt_hbm, wb_hbm,
                 xbuf, xn_ref, wbuf, wf32, a_st, zc_st, gt_st, ut_st, zt_st,
                 sem_x, sem_w, sem_wf, sem_ws, sem_o):
    j, nj = pl.program_id(0), pl.num_programs(0)
    e = wbuf.shape[2]
    h = ut_st.shape[1]
    not_first = j > 0

    def x_copy(jj):
        return pltpu.make_async_copy(x_hbm.at[:, :, jj, :], xbuf, sem_x.at[0])

    def section(ref, k):
        return ref.at[:, pl.ds(PROJ_SECTION[k] * e, e)]

    def w_copy(k, slot):
        return pltpu.make_async_copy(section(wb_hbm, k), wbuf.at[slot], sem_w.at[slot])

    def wf_copy(k):
        return pltpu.make_async_copy(section(wf_hbm, k), wf32, sem_wf.at[0])

    def ws_copy(k, slot):
        return pltpu.make_async_copy(wbuf.at[slot], section(wb_hbm, k), sem_ws.at[slot])

    def a_copy():
        return pltpu.make_async_copy(a_st, a_hbm.at[j], sem_o.at[0])

    def zc_copy():
        return pltpu.make_async_copy(zc_st, zc_hbm.at[j], sem_o.at[1])

    def gt_copy(g):
        return pltpu.make_async_copy(gt_st.at[g % 2], gt_hbm.at[j, :, pl.ds(g * e, e)], sem_o.at[2 + g % 2])

    def ut_copy():
        return pltpu.make_async_copy(ut_st, ut_hbm.at[:, pl.ds(j * h, h), :], sem_o.at[4])

    def zt_copy():
        return pltpu.make_async_copy(zt_st, zt_hbm.at[j], sem_o.at[5])

    def wait_previous(copy):
        @pl.when(not_first)
        def _():
            copy.wait()

    @pl.when(j == 0)
    def _():
        x_copy(0).start()
        wf_copy(0).start()

    x_copy(j).wait()
    _norm_slab(xbuf, gain_ref, xn_ref)

    @pl.when(j + 1 < nj)
    def _():
        x_copy(j + 1).start()

    xn = xn_ref

    def cast_section(slot):
        rb = 256

        def body(i, carry):
            r0 = pl.multiple_of(i * rb, rb)
            wbuf[slot, pl.ds(r0, rb), :] = wf32[pl.ds(r0, rb), :].astype(BF16)
            return carry

        lax.fori_loop(0, wf32.shape[0] // rb, body, 0)

    def acquire(k):
        slot = k % N_WSLOTS

        @pl.when(j == 0)
        def _():
            wf_copy(k).wait()
            if k >= N_WSLOTS:
                ws_copy(k - N_WSLOTS, slot).wait()
            cast_section(slot)
            if k + 1 < PROJ_STEPS:
                wf_copy(k + 1).start()
            ws_copy(k, slot).start()

        @pl.when(not_first)
        def _():
            w_copy(k, slot).wait()
            ahead = k + N_WSLOTS - 1
            if ahead < PROJ_STEPS:
                w_copy(ahead, ahead % N_WSLOTS).start()
            else:
                @pl.when(j + 1 < nj)
                def _():
                    w_copy(ahead - PROJ_STEPS, ahead % N_WSLOTS).start()

    def free_staging(k):
        if k == STEP_G:
            wait_previous(a_copy())
        elif k == STEP_ZC:
            wait_previous(zc_copy())
        elif STEP_GATE0 <= k < STEP_U:
            g = k - STEP_GATE0
            if g < 2:
                wait_previous(gt_copy(g))
            else:
                gt_copy(g).wait()
        elif k == STEP_U:
            wait_previous(ut_copy())
        elif k == STEP_ZS:
            wait_previous(zt_copy())

    def finish(k, acc):
        if k == STEP_G:
            a_st[...] = _sigmoid(acc)
        elif k == STEP_V:
            a_st[...] = a_st[...] * acc
            a_copy().start()
        elif k == STEP_ZC:
            zc_st[...] = _silu(acc).astype(BF16)
            zc_copy().start()
        elif k < STEP_U:
            g = k - STEP_GATE0
            gt_st[g % 2] = _sigmoid(acc + bg_ref[:, g * e:(g + 1) * e]).astype(BF16)
            gt_copy(g).start()
        elif k == STEP_U:
            ut_st[...] = acc.reshape(ut_st.shape).astype(BF16)
            ut_copy().start()
        else:
            zt_st[...] = _silu(acc).astype(BF16)
            zt_copy().start()

    for k in range(PROJ_STEPS):
        acquire(k)
        free_staging(k)
        w = wbuf[k % N_WSLOTS]
        finish(k, _dot_tt(w, xn[...]) if k >= STEP_U else _dot(xn[...], w))

    @pl.when((j == 0) & (nj > 1))
    def _():
        for k in range(PROJ_STEPS - N_WSLOTS, PROJ_STEPS):
            ws_copy(k, k % N_WSLOTS).wait()
        for k in range(N_WSLOTS - 1):
            w_copy(k, k).start()

    @pl.when((j == 0) & (nj == 1))
    def _():
        for k in range(PROJ_STEPS - N_WSLOTS, PROJ_STEPS):
            ws_copy(k, k % N_WSLOTS).wait()

    @pl.when(j == nj - 1)
    def _():
        a_copy().wait()
        zc_copy().wait()
        gt_copy(N_GATE_BLOCKS - 2).wait()
        gt_copy(N_GATE_BLOCKS - 1).wait()
        ut_copy().wait()
        zt_copy().wait()


def _proj_meta_kernel(m_ref, gain_ref, wv_ref, wg_ref, wu_ref, a_ref, u_ref):
    x = m_ref[...]
    ms = jnp.mean(x * x, axis=-1, keepdims=True)
    xn = (x * lax.rsqrt(ms + NORM_EPS) * gain_ref[...]).astype(BF16)
    a_ref[...] = _dot(xn, wv_ref[...]) * _sigmoid(_dot(xn, wg_ref[...]))
    u_ref[...] = _dot(xn, wu_ref[...])


def _lam_bar(lr, li, dt):
    lr = jnp.minimum(lr, LAM_RE_MAX)
    mag = jnp.exp(lr * dt)
    th = li * dt
    return lr, mag * jnp.cos(th), mag * jnp.sin(th)


def _s5_param_kernel(*refs):
    for g in range(refs[0].shape[0]):
        _s5_param_group(*[ref.at[g] for ref in refs])


def _s5_param_group(lrc_ref, lic_ref, lrr_ref, lir_ref, ls_ref, btr_ref, bti_ref,
                    ca_ref, cb_ref, m_ref, r_ref, o_ref, pw_ref):
    p = lrc_ref.shape[0]
    t = T_CHUNK
    h = btr_ref.shape[1]
    dt = jnp.exp(ls_ref[...])

    sel = (lax.broadcasted_iota(jnp.int32, (h, t * h), 1) % h
           == lax.broadcasted_iota(jnp.int32, (h, t * h), 0)).astype(F32)
    tile_b = lambda ref: jnp.dot(ref[...], sel, preferred_element_type=F32, precision=lax.Precision.HIGHEST)

    _, l1r, l1i = _lam_bar(lrr_ref[...], lir_ref[...], dt)
    eye = (lax.broadcasted_iota(jnp.int32, (p, 2 * p), 0) == lax.broadcasted_iota(jnp.int32, (p, 2 * p), 1))
    to_col = lambda row: jnp.sum(jnp.where(eye, row, 0.0), axis=1, keepdims=True)
    lam_r, lam_i = to_col(l1r), to_col(l1i)
    lr = jnp.minimum(lrc_ref[...], LAM_RE_MAX)
    li = lic_ref[...]
    den = lr * lr + li * li
    nr, ni = lam_r - 1.0, lam_i
    kr = (nr * lr + ni * li) / den
    ki = (ni * lr - nr * li) / den
    bbr, bbi = _cmul(kr, ki, tile_b(btr_ref), tile_b(bti_ref))

    lane = lax.broadcasted_iota(jnp.int32, (p, t * h), 1)
    expo = (t - 1) - lane // h
    pr = jnp.ones((p, t * h), F32)
    pi = jnp.zeros((p, t * h), F32)
    br_, bi_ = lam_r, lam_i
    for k in range(int(math.log2(t))):
        qr, qi = _cmul(pr, pi, br_, bi_)
        bit = ((expo >> k) & 1) == 1
        pr = jnp.where(bit, qr, pr)
        pi = jnp.where(bit, qi, pi)
        br_, bi_ = _cmul(br_, bi_, br_, bi_)
    wr, wi = _cmul(pr, pi, bbr, bbi)
    r_ref[0:p, :] = wr.astype(BF16)
    r_ref[p:2 * p, :] = wi.astype(BF16)

    ca = ca_ref[...]
    cb = cb_ref[...]
    sgn = jnp.where(lax.broadcasted_iota(jnp.int32, ca.shape, 1) < p, 1.0, -1.0)
    w = jnp.concatenate([wr, wi], axis=0)
    krev = jnp.dot(sgn * ca, w, preferred_element_type=F32, precision=lax.Precision.HIGHEST)

    half = LANES_V7X
    ka, kb = krev[:, :half], krev[:, half:]
    lane_h = lax.broadcasted_iota(jnp.int32, (h, half), 1)
    zero = jnp.zeros((h, half), F32)
    for i in range(t):
        s = (t - 1 - i) * h
        if s == 0:
            lo, hi = ka, kb
        elif s < half:
            ra = pltpu.roll(ka, half - s, 1)
            rb = pltpu.roll(kb, half - s, 1)
            keep = lane_h < (half - s)
            lo, hi = jnp.where(keep, ra, rb), jnp.where(keep, rb, zero)
        elif s == half:
            lo, hi = kb, zero
        else:
            rb = pltpu.roll(kb, 2 * half - s, 1)
            lo, hi = jnp.where(lane_h < (2 * half - s), rb, zero), zero
        m_ref[i * h:(i + 1) * h, 0:half] = lo.astype(BF16)
        m_ref[i * h:(i + 1) * h, half:2 * half] = hi.astype(BF16)

    qr, qi = l1r, l1i
    for i in range(t):
        o_ref[i * h:(i + 1) * h, :] = (sgn * (ca * qr) - cb * qi).astype(BF16)
        qr, qi = _cmul(qr, qi, l1r, l1i)

    ar, ai = l1r, l1i
    for _ in range(int(math.log2(t))):
        ar, ai = _cmul(ar, ai, ar, ai)
    qr, qi = ar, ai
    for s in range(SUBLANES_V7X):
        pw_ref[s:s + 1, :] = qr
        pw_ref[SUBLANES_V7X + s:SUBLANES_V7X + s + 1, :] = qi
        qr, qi = _cmul(qr, qi, ar, ai)


def _s5_scan_kernel(ut_ref, um_ref, m_ref, r_ref, o_ref, pw_ref, dcol_ref, yt_ref):
    ng, th, rows = ut_ref.shape
    p = r_ref.shape[1] // 2
    lw, sl = LANES_V7X, SUBLANES_V7X
    nb = rows // lw
    h = yt_ref.shape[1] // ng
    lane0 = lax.broadcasted_iota(jnp.int32, (th, lw), 1) == 0

    ys, zre, zim = [], [], []
    for g in range(ng):
        meta_blk = jnp.where(lane0, um_ref[g], 0.0).astype(BF16)
        ue = jnp.concatenate([ut_ref[g], meta_blk], axis=1)
        ys.append(_dot(m_ref[g], ut_ref[g]))
        z = _dot(r_ref[g], ue)
        zre.append(z[:p])
        zim.append(z[p:])
    zr = jnp.concatenate(zre, axis=0)
    zi = jnp.concatenate(zim, axis=0)

    def chunk_major(z):
        return jnp.concatenate([z[:, b * lw:(b + 1) * lw].T for b in range(nb)], axis=1)

    def tile_lanes(v):
        return jnp.concatenate([v] * nb, axis=1)

    er, ei = chunk_major(zr), chunk_major(zi)
    car_r = tile_lanes(zr[:, rows:rows + lw].T[0:1, :])
    car_i = tile_lanes(zi[:, rows:rows + lw].T[0:1, :])
    pw = tile_lanes(pw_ref[...])
    nc, width = er.shape
    srow = lax.broadcasted_iota(jnp.int32, (nc, width), 0) % sl

    def tile_roll(x, sh):
        return pltpu.roll(x.reshape(nc // sl, sl, width), sh, 1).reshape(nc, width)

    for sh in (1, 2, 4):
        dr, di = _cmul(pw[sh - 1:sh], pw[sl + sh - 1:sl + sh], tile_roll(er, sh), tile_roll(ei, sh))
        ok = srow >= sh
        er = er + jnp.where(ok, dr, 0.0)
        ei = ei + jnp.where(ok, di, 0.0)

    a8r, a8i = pw[0:sl], pw[sl:2 * sl]
    first = lax.broadcasted_iota(jnp.int32, (sl, width), 0) == 0
    xr_rows, xi_rows = [], []
    for r in range(nc // sl):
        cr, ci = _cmul(a8r, a8i, car_r, car_i)
        fr = er[r * sl:(r + 1) * sl] + cr
        fi = ei[r * sl:(r + 1) * sl] + ci
        xr_rows.append(jnp.where(first, car_r, pltpu.roll(fr, 1, 0)))
        xi_rows.append(jnp.where(first, car_i, pltpu.roll(fi, 1, 0)))
        car_r, car_i = fr[sl - 1:sl], fi[sl - 1:sl]
    sr = jnp.concatenate(xr_rows, axis=0)
    si = jnp.concatenate(xi_rows, axis=0)

    def state_major(s):
        return jnp.concatenate([s[:, b * lw:(b + 1) * lw].T for b in range(nb)], axis=1)

    srt, sit = state_major(sr), state_major(si)
    for g in range(ng):
        s_in = jnp.concatenate([srt[g * p:(g + 1) * p], sit[g * p:(g + 1) * p]], axis=0).astype(BF16)
        y = ys[g] + _dot(o_ref[g], s_in) + dcol_ref[g] * ut_ref[g].astype(F32)
        yt_ref[:, g * h:(g + 1) * h, :] = y.reshape(th // h, h, rows).astype(BF16)


def _conv_branch_kernel(a_ref, halo_ref, am_ref, zc_ref, gc_ref, w_ref, b_ref, lng_ref, lnb_ref,
                        wc_ref, o_ref, s1_ref, s2_ref, yc_ref, cv_ref, *, width, ranges_per_seq):
    t, nr, e = a_ref.shape
    hr = halo_ref.shape[1]
    d = o_ref.shape[-1]
    seq_start = (pl.program_id(0) % ranges_per_seq) == 0
    for j in range(t):
        p1 = jnp.where(seq_start, am_ref[j:j + 1, :], halo_ref[j, hr - 1:hr, :])
        p2 = jnp.where(seq_start, 0.0, halo_ref[j, hr - 2:hr - 1, :])
        s1_ref[j, 0:1, :] = p1
        s1_ref[j, 1:nr, :] = a_ref[j, 0:nr - 1, :]
        s2_ref[j, 0:1, :] = p2
        s2_ref[j, 1:2, :] = p1
        s2_ref[j, 2:nr, :] = a_ref[j, 0:nr - 2, :]
    srcs = (a_ref, s1_ref, s2_ref)
    lng, lnb = lng_ref[...], lnb_ref[...]
    lw = LANES_V7X
    sl = w_ref.shape[1]
    for c0 in range(0, e, lw):
        wk = [w_ref[k, :, c0:c0 + lw] for k in range(width)]
        bias = jnp.broadcast_to(b_ref[:, c0:c0 + lw], (sl, lw))

        def row_tile(i, carry, c0=c0, wk=wk, bias=bias):
            r0 = pl.multiple_of(i * sl, sl)
            for jo in range(t):
                acc = bias
                for s in range(width):
                    r, q = s % t, s // t
                    src = srcs[q + (1 if r > jo else 0)]
                    acc = acc + wk[width - 1 - s] * src[(jo - r) % t, pl.ds(r0, sl), c0:c0 + lw]
                cv_ref[jo, pl.ds(r0, sl), c0:c0 + lw] = acc
            return carry

        lax.fori_loop(0, nr // sl, row_tile, 0, unroll=4)
    n_parts = 4
    th = t // n_parts
    for part in range(n_parts):
        for jo in range(part * th, (part + 1) * th):
            acc = cv_ref[jo]
            mu = jnp.mean(acc, axis=-1, keepdims=True)
            xc = acc - mu
            var = jnp.mean(xc * xc, axis=-1, keepdims=True)
            yn = xc * lax.rsqrt(var + LN_EPS) * lng + lnb
            yc_ref[jo * nr:(jo + 1) * nr, :] = (_silu(yn) * zc_ref[jo].astype(F32)).astype(BF16)
        rows = slice(part * th * nr, (part + 1) * th * nr)
        slabs = slice(part * th, (part + 1) * th)
        pc = _dot(yc_ref[rows, :], wc_ref[...])
        gate = gc_ref[slabs].reshape(th * nr, d).astype(F32)
        o_ref[slabs] = (gate * pc).astype(BF16).reshape(th, nr, d)


def _ssm_out_kernel(yt_ref, zt_ref, mc_ref, gs_ref, wg_ref, bcol_ref, ws_ref, w_ref, gain_ref,
                    x_hbm, o_hbm, xbuf, obuf, sem_in, sem_out):
    j, r = pl.program_id(0), pl.program_id(1)
    nr = pl.num_programs(1)
    step, n_steps = j * nr + r, pl.num_programs(0) * nr
    _, hb, nc, d = xbuf.shape
    slot = step % 2

    def x_copy(jj, rr, sl):
        return pltpu.make_async_copy(x_hbm.at[pl.ds(rr * hb, hb), :, jj, :], xbuf.at[sl], sem_in.at[sl])

    def o_copy(jj, rr, sl):
        return pltpu.make_async_copy(obuf.at[sl], o_hbm.at[pl.ds(rr * hb, hb), :, jj, :], sem_out.at[sl])

    @pl.when(step == 0)
    def _():
        x_copy(0, 0, 0).start()

    x_copy(j, r, slot).wait()

    @pl.when(step + 1 < n_steps)
    def _():
        nxt = step + 1
        x_copy(nxt // nr, nxt % nr, 1 - slot).start()

    @pl.when(step >= 2)
    def _():
        o_copy(j, r, slot).wait()

    y = jax.nn.gelu(yt_ref[...].astype(F32), approximate=True)
    g = _dot(wg_ref[...], y.astype(BF16)) + bcol_ref[...]
    y2t = (y * _sigmoid(g) * zt_ref[...].astype(F32)).astype(BF16)
    ps = _dot_tn(y2t, ws_ref[...])
    merged = (mc_ref[...].astype(F32) + gs_ref[...].astype(F32) * ps).astype(BF16)
    hres = xbuf[slot].reshape(hb * nc, d) + _dot(merged, w_ref[...])
    ms = jnp.mean(hres * hres, axis=-1, keepdims=True)
    obuf[slot] = (hres * lax.rsqrt(ms + NORM_EPS) * gain_ref[...]).reshape(hb, nc, d)
    o_copy(j, r, slot).start()

    @pl.when(step == n_steps - 1)
    def _():
        o_copy(j, r, 1 - slot).wait()
        o_copy(j, r, slot).wait()


def _col_tile(n):
    return 512 if n % 512 == 0 else n


def kernel(x, meta, norm_g, w_in, b_gate, dw_w, dw_b, ln_g, ln_b, w_conv, lam_re, lam_im,
           log_step, b_re, b_im, c_re, c_im, d_skip, w_glu, b_glu, w_ssm, w_out, final_g):
    bsz, seq, d = x.shape
    t = T_CHUNK
    e = w_conv.shape[0]
    g_n, p_n = lam_re.shape
    h_n = e // g_n
    width = dw_w.shape[0]
    nc = seq // t
    rows = bsz * nc
    assert meta.shape[0] == t and h_n * t == 2 * LANES_V7X and nc == LANES_V7X
    assert width <= 2 * t - 1 and seq % t == 0 and 2 * p_n == LANES_V7X and g_n % 2 == 0
    tn = _col_tile(e)
    ne = e // tn

    x4 = x.reshape(bsz, nc, t, d)
    gain = norm_g.reshape(1, d)

    assert 2 * d == N_GATE_BLOCKS * e and w_in.shape[1] == PROJ_STEPS * e and PROJ_STEPS % N_WSLOTS == 0
    hbm = pl.BlockSpec(memory_space=pl.ANY)
    a_slab, zc_slab, gates, ut, zst, w_in_b = pl.pallas_call(
        _proj_kernel,
        grid=(t,),
        in_specs=[hbm, hbm,
                  pl.BlockSpec((1, d), lambda j: (0, 0)),
                  pl.BlockSpec((1, 2 * d), lambda j: (0, 0))],
        out_specs=[hbm] * 6,
        out_shape=[jax.ShapeDtypeStruct((t, rows, e), F32),
                   jax.ShapeDtypeStruct((t, rows, e), BF16),
                   jax.ShapeDtypeStruct((t, rows, 2 * d), BF16),
                   jax.ShapeDtypeStruct((g_n, t * h_n, rows), BF16),
                   jax.ShapeDtypeStruct((t, e, rows), BF16),
                   jax.ShapeDtypeStruct(w_in.shape, BF16)],
        scratch_shapes=[pltpu.VMEM((bsz, nc, d), F32), pltpu.VMEM((rows, d), BF16),
                        pltpu.VMEM((N_WSLOTS, d, e), BF16), pltpu.VMEM((d, e), F32),
                        pltpu.VMEM((rows, e), F32), pltpu.VMEM((rows, e), BF16),
                        pltpu.VMEM((2, rows, e), BF16), pltpu.VMEM((g_n, h_n, rows), BF16),
                        pltpu.VMEM((e, rows), BF16),
                        pltpu.SemaphoreType.DMA((1,)), pltpu.SemaphoreType.DMA((N_WSLOTS,)),
                        pltpu.SemaphoreType.DMA((1,)), pltpu.SemaphoreType.DMA((N_WSLOTS,)),
                        pltpu.SemaphoreType.DMA((6,))],
        compiler_params=_cparams("arbitrary"),
        name="proj",
    )(x4, w_in, gain, b_gate.reshape(1, 2 * d))

    a_meta, u_meta = pl.pallas_call(
        _proj_meta_kernel,
        grid=(ne,),
        in_specs=[pl.BlockSpec((t, d), lambda n: (0, 0)), pl.BlockSpec((1, d), lambda n: (0, 0)),
                  pl.BlockSpec((d, tn), lambda n: (0, n)), pl.BlockSpec((d, tn), lambda n: (0, ne + n)),
                  pl.BlockSpec((d, tn), lambda n: (0, 3 * ne + n))],
        out_specs=[pl.BlockSpec((t, tn), lambda n: (0, n)), pl.BlockSpec((t, tn), lambda n: (0, n))],
        out_shape=[jax.ShapeDtypeStruct((t, e), F32), jax.ShapeDtypeStruct((t, e), F32)],
        compiler_params=_cparams("arbitrary"),
        name="proj_meta",
    )(meta, gain, w_in_b, w_in_b, w_in_b)
    um_col = u_meta.reshape(t, g_n, h_n).transpose(1, 0, 2).reshape(g_n, t * h_n, 1)

    grp = lambda *shape: pl.BlockSpec((None,) + shape, lambda g: (g,) + (0,) * len(shape))
    tile2 = lambda v: jnp.concatenate([v, v], axis=-1)
    gpp = 8
    gblk = lambda *shape: pl.BlockSpec((gpp,) + shape, lambda g: (g,) + (0,) * len(shape))
    m_op, r_op, o_op, a_pow = pl.pallas_call(
        _s5_param_kernel,
        grid=(g_n // gpp,),
        in_specs=[gblk(p_n, 1), gblk(p_n, 1), gblk(1, 2 * p_n), gblk(1, 2 * p_n), gblk(1, 1),
                  gblk(p_n, h_n), gblk(p_n, h_n), gblk(h_n, 2 * p_n), gblk(h_n, 2 * p_n)],
        out_specs=[gblk(t * h_n, t * h_n), gblk(2 * p_n, t * h_n), gblk(t * h_n, 2 * p_n),
                   gblk(2 * SUBLANES_V7X, 2 * p_n)],
        out_shape=[jax.ShapeDtypeStruct((g_n, t * h_n, t * h_n), BF16),
                   jax.ShapeDtypeStruct((g_n, 2 * p_n, t * h_n), BF16),
                   jax.ShapeDtypeStruct((g_n, t * h_n, 2 * p_n), BF16),
                   jax.ShapeDtypeStruct((g_n, 2 * SUBLANES_V7X, 2 * p_n), F32)],
        compiler_params=_cparams("parallel"),
        name="s5_params",
    )(lam_re.reshape(g_n, p_n, 1), lam_im.reshape(g_n, p_n, 1),
      tile2(lam_re).reshape(g_n, 1, 2 * p_n), tile2(lam_im).reshape(g_n, 1, 2 * p_n),
      log_step.reshape(g_n, 1, 1),
      b_re, b_im,
      jnp.concatenate([c_re, c_im], axis=-1), jnp.concatenate([c_im, c_re], axis=-1))
    pw_pair = (a_pow[:, :, :p_n].reshape(g_n // 2, 2, 2 * SUBLANES_V7X, p_n)
               .transpose(0, 2, 1, 3).reshape(g_n // 2, 2 * SUBLANES_V7X, 2 * p_n))

    dcol = jnp.tile(d_skip.reshape(g_n, 1, h_n), (1, t, 1)).reshape(g_n, t * h_n, 1)
    pair = lambda *shape: pl.BlockSpec((2,) + shape, lambda g: (g,) + (0,) * len(shape))
    yt = pl.pallas_call(
        _s5_scan_kernel,
        grid=(g_n // 2,),
        in_specs=[pair(t * h_n, rows), pair(t * h_n, 1), pair(t * h_n, t * h_n), pair(2 * p_n, t * h_n),
                  pair(t * h_n, 2 * p_n), grp(2 * SUBLANES_V7X, 2 * p_n), pair(t * h_n, 1)],
        out_specs=pl.BlockSpec((t, 2 * h_n, rows), lambda g: (0, g, 0)),
        out_shape=jax.ShapeDtypeStruct((t, e, rows), BF16),
        compiler_params=_cparams("parallel"),
        name="s5_scan",
    )(ut, um_col, m_op, r_op, o_op, pw_pair, dcol)

    nr = 64
    hr = SUBLANES_V7X
    ranges_per_seq = nc // nr
    mc = pl.pallas_call(
        functools.partial(_conv_branch_kernel, width=width, ranges_per_seq=ranges_per_seq),
        grid=(rows // nr,),
        in_specs=[pl.BlockSpec((t, nr, e), lambda q: (0, q, 0)),
                  pl.BlockSpec((t, hr, e), lambda q: (0, jnp.maximum(q * (nr // hr) - 1, 0), 0)),
                  pl.BlockSpec((t, e), lambda q: (0, 0)),
                  pl.BlockSpec((t, nr, e), lambda q: (0, q, 0)),
                  pl.BlockSpec((t, nr, d), lambda q: (0, q, 0)),
                  pl.BlockSpec((width, SUBLANES_V7X, e), lambda q: (0, 0, 0)),
                  pl.BlockSpec((1, e), lambda q: (0, 0)),
                  pl.BlockSpec((1, e), lambda q: (0, 0)),
                  pl.BlockSpec((1, e), lambda q: (0, 0)),
                  pl.BlockSpec((e, d), lambda q: (0, 0))],
        out_specs=pl.BlockSpec((t, nr, d), lambda q: (0, q, 0)),
        out_shape=jax.ShapeDtypeStruct((t, rows, d), BF16),
        scratch_shapes=[pltpu.VMEM((t, nr, e), F32), pltpu.VMEM((t, nr, e), F32),
                        pltpu.VMEM((t * nr, e), BF16), pltpu.VMEM((t, nr, e), F32)],
        compiler_params=_cparams("parallel"),
        name="conv_branch",
    )(a_slab, a_slab, a_meta, zc_slab, gates,
      jnp.broadcast_to(dw_w.reshape(width, 1, e), (width, SUBLANES_V7X, e)), dw_b.reshape(1, e),
      ln_g.reshape(1, e), ln_b.reshape(1, e), w_conv.astype(BF16))

    rt = rows // 2
    hb = bsz // 2
    const = lambda *shape: pl.BlockSpec(shape, lambda j, r: (0,) * len(shape), pipeline_mode=pl.Buffered(1))
    out = pl.pallas_call(
        _ssm_out_kernel,
        grid=(t, 2),
        in_specs=[pl.BlockSpec((None, e, rt), lambda j, r: (j, 0, r)),
                  pl.BlockSpec((None, e, rt), lambda j, r: (j, 0, r)),
                  pl.BlockSpec((None, rt, d), lambda j, r: (j, r, 0)),
                  pl.BlockSpec((None, rt, d), lambda j, r: (j, r, 1)),
                  const(e, e), const(e, 1), const(e, d), const(d, d), const(1, d),
                  pl.BlockSpec(memory_space=pl.ANY)],
        out_specs=pl.BlockSpec(memory_space=pl.ANY),
        out_shape=jax.ShapeDtypeStruct((bsz, nc, t, d), F32),
        scratch_shapes=[pltpu.VMEM((2, hb, nc, d), F32), pltpu.VMEM((2, hb, nc, d), F32),
                        pltpu.SemaphoreType.DMA((2,)), pltpu.SemaphoreType.DMA((2,))],
        compiler_params=_cparams("arbitrary", "arbitrary"),
        name="ssm_out",
    )(yt, zst, mc, gates, w_glu.T.astype(BF16), b_glu.reshape(e, 1), w_ssm.astype(BF16),
      w_out.astype(BF16), final_g.reshape(1, d), x4)
    return out.reshape(bsz, seq, d)
```

```python
import functools
import math

import jax
import jax.numpy as jnp
from jax import lax
from jax.experimental import pallas as pl
from jax.experimental.pallas import tpu as pltpu

F32 = jnp.float32
BF16 = jnp.bfloat16

T_CHUNK = 16
LANES_V7X = 128
SUBLANES_V7X = 8
NORM_EPS = 1e-6
LN_EPS = 1e-5
LAM_RE_MAX = -1e-4
VMEM_LIMIT_V7X = 56 * 1024 * 1024


def _cparams(*sem):
    return pltpu.CompilerParams(dimension_semantics=sem, vmem_limit_bytes=VMEM_LIMIT_V7X)


def _sigmoid(x):
    return 0.5 * jnp.tanh(0.5 * x) + 0.5


def _silu(x):
    return x * _sigmoid(x)


def _dot(a, b):
    return jnp.dot(a, b, preferred_element_type=F32)


def _dot_nt(a, b):
    return lax.dot_general(a, b, (((1,), (1,)), ((), ())), preferred_element_type=F32)


def _dot_tn(a, b):
    return lax.dot_general(a, b, (((0,), (0,)), ((), ())), preferred_element_type=F32)


def _dot_tt(a, b):
    return lax.dot_general(a, b, (((0,), (1,)), ((), ())), preferred_element_type=F32)


def _cmul(ar, ai, br, bi):
    return ar * br - ai * bi, ar * bi + ai * br


def _norm_slab(x_ref, gain_ref, xn_ref):
    nb, nc, _ = x_ref.shape
    gain = gain_ref[...]

    def body(b, carry):
        x = x_ref[b]
        ms = jnp.mean(x * x, axis=-1, keepdims=True)
        r0 = pl.multiple_of(b * nc, nc)
        xn_ref[pl.ds(r0, nc), :] = (x * lax.rsqrt(ms + NORM_EPS) * gain).astype(BF16)
        return carry

    lax.fori_loop(0, nb, body, 0)


PROJ_STEPS = 9
N_WSLOTS = 3
PROJ_SECTION = (1, 0, 2, 5, 6, 7, 8, 3, 4)
STEP_G, STEP_V, STEP_ZC, STEP_GATE0, STEP_U, STEP_ZS = 0, 1, 2, 3, 7, 8
N_GATE_BLOCKS = STEP_U - STEP_GATE0


def _proj_kernel(x_hbm, wf_hbm, gain_ref, bg_ref, a_hbm, zc_hbm, gt_hbm, ut_hbm, zt_hbm, wb_hbm,
                 xbuf, xn_ref, wbuf, wf32, a_st, zc_st, gt_st, ut_st, zt_st,
                 sem_x, sem_w, sem_wf, sem_ws, sem_o):
    j, nj = pl.program_id(0), pl.num_programs(0)
    e = wbuf.shape[2]
    h = ut_st.shape[1]
    not_first = j > 0

    def x_copy(jj):
        return pltpu.make_async_copy(x_hbm.at[:, :, jj, :], xbuf, sem_x.at[0])

    def section(ref, k):
        return ref.at[:, pl.ds(PROJ_SECTION[k] * e, e)]

    def w_copy(k, slot):
        return pltpu.make_async_copy(section(wb_hbm, k), wbuf.at[slot], sem_w.at[slot])

    def wf_copy(k):
        return pltpu.make_async_copy(section(wf_hbm, k), wf32, sem_wf.at[0])

    def ws_copy(k, slot):
        return pltpu.make_async_copy(wbuf.at[slot], section(wb_hbm, k), sem_ws.at[slot])

    def a_copy():
        return pltpu.make_async_copy(a_st, a_hbm.at[j], sem_o.at[0])

    def zc_copy():
        return pltpu.make_async_copy(zc_st, zc_hbm.at[j], sem_o.at[1])

    def gt_copy(g):
        return pltpu.make_async_copy(gt_st.at[g % 2], gt_hbm.at[j, :, pl.ds(g * e, e)], sem_o.at[2 + g % 2])

    def ut_copy():
        return pltpu.make_async_copy(ut_st, ut_hbm.at[:, pl.ds(j * h, h), :], sem_o.at[4])

    def zt_copy():
        return pltpu.make_async_copy(zt_st, zt_hbm.at[j], sem_o.at[5])

    def wait_previous(copy):
        @pl.when(not_first)
        def _():
            copy.wait()

    @pl.when(j == 0)
    def _():
        x_copy(0).start()
        wf_copy(0).start()

    x_copy(j).wait()
    _norm_slab(xbuf, gain_ref, xn_ref)

    @pl.when(j + 1 < nj)
    def _():
        x_copy(j + 1).start()

    xn = xn_ref

    def cast_section(slot):
        rb = 256

        def body(i, carry):
            r0 = pl.multiple_of(i * rb, rb)
            wbuf[slot, pl.ds(r0, rb), :] = wf32[pl.ds(r0, rb), :].astype(BF16)
            return carry

        lax.fori_loop(0, wf32.shape[0] // rb, body, 0)

    def acquire(k):
        slot = k % N_WSLOTS

        @pl.when(j == 0)
        def _():
            wf_copy(k).wait()
            if k >= N_WSLOTS:
                ws_copy(k - N_WSLOTS, slot).wait()
            cast_section(slot)
            if k + 1 < PROJ_STEPS:
                wf_copy(k + 1).start()
            ws_copy(k, slot).start()

        @pl.when(not_first)
        def _():
            w_copy(k, slot).wait()
            ahead = k + N_WSLOTS - 1
            if ahead < PROJ_STEPS:
                w_copy(ahead, ahead % N_WSLOTS).start()
            else:
                @pl.when(j + 1 < nj)
                def _():
                    w_copy(ahead - PROJ_STEPS, ahead % N_WSLOTS).start()

    def free_staging(k):
        if k == STEP_G:
            wait_previous(a_copy())
        elif k == STEP_ZC:
            wait_previous(zc_copy())
        elif STEP_GATE0 <= k < STEP_U:
            g = k - STEP_GATE0
            if g < 2:
                wait_previous(gt_copy(g))
            else:
                gt_copy(g).wait()
        elif k == STEP_U:
            wait_previous(ut_copy())
        elif k == STEP_ZS:
            wait_previous(zt_copy())

    def finish(k, acc):
        if k == STEP_G:
            a_st[...] = _sigmoid(acc)
        elif k == STEP_V:
            a_st[...] = a_st[...] * acc
            a_copy().start()
        elif k == STEP_ZC:
            zc_st[...] = _silu(acc).astype(BF16)
            zc_copy().start()
        elif k < STEP_U:
            g = k - STEP_GATE0
            gt_st[g % 2] = _sigmoid(acc + bg_ref[:, g * e:(g + 1) * e]).astype(BF16)
            gt_copy(g).start()
        elif k == STEP_U:
            ut_st[...] = acc.reshape(ut_st.shape).astype(BF16)
            ut_copy().start()
        else:
            zt_st[...] = _silu(acc).astype(BF16)
            zt_copy().start()

    for k in range(PROJ_STEPS):
        acquire(k)
        free_staging(k)
        w = wbuf[k % N_WSLOTS]
        finish(k, _dot_tt(w, xn[...]) if k >= STEP_U else _dot(xn[...], w))

    @pl.when((j == 0) & (nj > 1))
    def _():
        for k in range(PROJ_STEPS - N_WSLOTS, PROJ_STEPS):
            ws_copy(k, k % N_WSLOTS).wait()
        for k in range(N_WSLOTS - 1):
            w_copy(k, k).start()

    @pl.when((j == 0) & (nj == 1))
    def _():
        for k in range(PROJ_STEPS - N_WSLOTS, PROJ_STEPS):
            ws_copy(k, k % N_WSLOTS).wait()

    @pl.when(j == nj - 1)
    def _():
        a_copy().wait()
        zc_copy().wait()
        gt_copy(N_GATE_BLOCKS - 2).wait()
        gt_copy(N_GATE_BLOCKS - 1).wait()
        ut_copy().wait()
        zt_copy().wait()


def _proj_meta_kernel(m_ref, gain_ref, wv_ref, wg_ref, wu_ref, a_ref, u_ref):
    x = m_ref[...]
    ms = jnp.mean(x * x, axis=-1, keepdims=True)
    xn = (x * lax.rsqrt(ms + NORM_EPS) * gain_ref[...]).astype(BF16)
    a_ref[...] = _dot(xn, wv_ref[...]) * _sigmoid(_dot(xn, wg_ref[...]))
    u_ref[...] = _dot(xn, wu_ref[...])


def _lam_bar(lr, li, dt):
    lr = jnp.minimum(lr, LAM_RE_MAX)
    mag = jnp.exp(lr * dt)
    th = li * dt
    return lr, mag * jnp.cos(th), mag * jnp.sin(th)


def _s5_param_kernel(*refs):
    for g in range(refs[0].shape[0]):
        _s5_param_group(*[ref.at[g] for ref in refs])


def _s5_param_group(lrc_ref, lic_ref, lrr_ref, lir_ref, ls_ref, btr_ref, bti_ref,
                    ca_ref, cb_ref, m_ref, r_ref, o_ref, pw_ref):
    p = lrc_ref.shape[0]
    t = T_CHUNK
    h = btr_ref.shape[1]
    dt = jnp.exp(ls_ref[...])

    sel = (lax.broadcasted_iota(jnp.int32, (h, t * h), 1) % h
           == lax.broadcasted_iota(jnp.int32, (h, t * h), 0)).astype(F32)
    tile_b = lambda ref: jnp.dot(ref[...], sel, preferred_element_type=F32, precision=lax.Precision.HIGHEST)

    _, l1r, l1i = _lam_bar(lrr_ref[...], lir_ref[...], dt)
    eye = (lax.broadcasted_iota(jnp.int32, (p, 2 * p), 0) == lax.broadcasted_iota(jnp.int32, (p, 2 * p), 1))
    to_col = lambda row: jnp.sum(jnp.where(eye, row, 0.0), axis=1, keepdims=True)
    lam_r, lam_i = to_col(l1r), to_col(l1i)
    lr = jnp.minimum(lrc_ref[...], LAM_RE_MAX)
    li = lic_ref[...]
    den = lr * lr + li * li
    nr, ni = lam_r - 1.0, lam_i
    kr = (nr * lr + ni * li) / den
    ki = (ni * lr - nr * li) / den
    bbr, bbi = _cmul(kr, ki, tile_b(btr_ref), tile_b(bti_ref))

    lane = lax.broadcasted_iota(jnp.int32, (p, t * h), 1)
    expo = (t - 1) - lane // h
    pr = jnp.ones((p, t * h), F32)
    pi = jnp.zeros((p, t * h), F32)
    br_, bi_ = lam_r, lam_i
    for k in range(int(math.log2(t))):
        qr, qi = _cmul(pr, pi, br_, bi_)
        bit = ((expo >> k) & 1) == 1
        pr = jnp.where(bit, qr, pr)
        pi = jnp.where(bit, qi, pi)
        br_, bi_ = _cmul(br_, bi_, br_, bi_)
    wr, wi = _cmul(pr, pi, bbr, bbi)
    r_ref[0:p, :] = wr.astype(BF16)
    r_ref[p:2 * p, :] = wi.astype(BF16)

    ca = ca_ref[...]
    cb = cb_ref[...]
    sgn = jnp.where(lax.broadcasted_iota(jnp.int32, ca.shape, 1) < p, 1.0, -1.0)
    w = jnp.concatenate([wr, wi], axis=0)
    krev = jnp.dot(sgn * ca, w, preferred_element_type=F32, precision=lax.Precision.HIGHEST)

    half = LANES_V7X
    ka, kb = krev[:, :half], krev[:, half:]
    lane_h = lax.broadcasted_iota(jnp.int32, (h, half), 1)
    zero = jnp.zeros((h, half), F32)
    for i in range(t):
        s = (t - 1 - i) * h
        if s == 0:
            lo, hi = ka, kb
        elif s < half:
            ra = pltpu.roll(ka, half - s, 1)
            rb = pltpu.roll(kb, half - s, 1)
            keep = lane_h < (half - s)
            lo, hi = jnp.where(keep, ra, rb), jnp.where(keep, rb, zero)
        elif s == half:
            lo, hi = kb, zero
        else:
            rb = pltpu.roll(kb, 2 * half - s, 1)
            lo, hi = jnp.where(lane_h < (2 * half - s), rb, zero), zero
        m_ref[i * h:(i + 1) * h, 0:half] = lo.astype(BF16)
        m_ref[i * h:(i + 1) * h, half:2 * half] = hi.astype(BF16)

    qr, qi = l1r, l1i
    for i in range(t):
        o_ref[i * h:(i + 1) * h, :] = (sgn * (ca * qr) - cb * qi).astype(BF16)
        qr, qi = _cmul(qr, qi, l1r, l1i)

    ar, ai = l1r, l1i
    for _ in range(int(math.log2(t))):
        ar, ai = _cmul(ar, ai, ar, ai)
    qr, qi = ar, ai
    for s in range(SUBLANES_V7X):
        pw_ref[s:s + 1, :] = qr
        pw_ref[SUBLANES_V7X + s:SUBLANES_V7X + s + 1, :] = qi
        qr, qi = _cmul(qr, qi, ar, ai)


def _s5_scan_kernel(ut_ref, um_ref, m_ref, r_ref, o_ref, pw_ref, dcol_ref, yt_ref):
    ng, th, rows = ut_ref.shape
    p = r_ref.shape[1] // 2
    lw, sl = LANES_V7X, SUBLANES_V7X
    nb = rows // lw
    h = yt_ref.shape[1] // ng
    lane0 = lax.broadcasted_iota(jnp.int32, (th, lw), 1) == 0

    ys, zre, zim = [], [], []
    for g in range(ng):
        meta_blk = jnp.where(lane0, um_ref[g], 0.0).astype(BF16)
        ue = jnp.concatenate([ut_ref[g], meta_blk], axis=1)
        ys.append(_dot(m_ref[g], ut_ref[g]))
        z = _dot(r_ref[g], ue)
        zre.append(z[:p])
        zim.append(z[p:])
    zr = jnp.concatenate(zre, axis=0)
    zi = jnp.concatenate(zim, axis=0)

    def chunk_major(z):
        return jnp.concatenate([z[:, b * lw:(b + 1) * lw].T for b in range(nb)], axis=1)

    def tile_lanes(v):
        return jnp.concatenate([v] * nb, axis=1)

    er, ei = chunk_major(zr), chunk_major(zi)
    car_r = tile_lanes(zr[:, rows:rows + lw].T[0:1, :])
    car_i = tile_lanes(zi[:, rows:rows + lw].T[0:1, :])
    pw = tile_lanes(pw_ref[...])
    nc, width = er.shape
    srow = lax.broadcasted_iota(jnp.int32, (nc, width), 0) % sl

    def tile_roll(x, sh):
        return pltpu.roll(x.reshape(nc // sl, sl, width), sh, 1).reshape(nc, width)

    for sh in (1, 2, 4):
        dr, di = _cmul(pw[sh - 1:sh], pw[sl + sh - 1:sl + sh], tile_roll(er, sh), tile_roll(ei, sh))
        ok = srow >= sh
        er = er + jnp.where(ok, dr, 0.0)
        ei = ei + jnp.where(ok, di, 0.0)

    a8r, a8i = pw[0:sl], pw[sl:2 * sl]
    first = lax.broadcasted_iota(jnp.int32, (sl, width), 0) == 0
    xr_rows, xi_rows = [], []
    for r in range(nc // sl):
        cr, ci = _cmul(a8r, a8i, car_r, car_i)
        fr = er[r * sl:(r + 1) * sl] + cr
        fi = ei[r * sl:(r + 1) * sl] + ci
        xr_rows.append(jnp.where(first, car_r, pltpu.roll(fr, 1, 0)))
        xi_rows.append(jnp.where(first, car_i, pltpu.roll(fi, 1, 0)))
        car_r, car_i = fr[sl - 1:sl], fi[sl - 1:sl]
    sr = jnp.concatenate(xr_rows, axis=0)
    si = jnp.concatenate(xi_rows, axis=0)

    def state_major(s):
        sw = ng * p
        return jnp.concatenate([s[:, b * sw:(b + 1) * sw].T for b in range(nb)], axis=1)

    srt, sit = state_major(sr), state_major(si)
    for g in range(ng):
        s_in = jnp.concatenate([srt[g * p:(g + 1) * p], sit[g * p:(g + 1) * p]], axis=0).astype(BF16)
        y = ys[g] + _dot(o_ref[g], s_in) + dcol_ref[g] * ut_ref[g].astype(F32)
        yt_ref[:, g * h:(g + 1) * h, :] = y.reshape(th // h, h, rows).astype(BF16)


def _conv_branch_kernel(a_ref, halo_ref, am_ref, zc_ref, gc_ref, w_ref, b_ref, lng_ref, lnb_ref,
                        wc_ref, o_ref, s1_ref, s2_ref, yc_ref, cv_ref, *, width, ranges_per_seq):
    t, nr, e = a_ref.shape
    hr = halo_ref.shape[1]
    d = o_ref.shape[-1]
    seq_start = (pl.program_id(0) % ranges_per_seq) == 0
    for j in range(t):
        p1 = jnp.where(seq_start, am_ref[j:j + 1, :], halo_ref[j, hr - 1:hr, :])
        p2 = jnp.where(seq_start, 0.0, halo_ref[j, hr - 2:hr - 1, :])
        s1_ref[j, 0:1, :] = p1
        s1_ref[j, 1:nr, :] = a_ref[j, 0:nr - 1, :]
        s2_ref[j, 0:1, :] = p2
        s2_ref[j, 1:2, :] = p1
        s2_ref[j, 2:nr, :] = a_ref[j, 0:nr - 2, :]
    srcs = (a_ref, s1_ref, s2_ref)
    lng, lnb = lng_ref[...], lnb_ref[...]
    lw = LANES_V7X
    sl = w_ref.shape[1]
    for c0 in range(0, e, lw):
        wk = [w_ref[k, :, c0:c0 + lw] for k in range(width)]
        bias = jnp.broadcast_to(b_ref[:, c0:c0 + lw], (sl, lw))

        def row_tile(i, carry, c0=c0, wk=wk, bias=bias):
            r0 = pl.multiple_of(i * sl, sl)
            for jo in range(t):
                acc = bias
                for s in range(width):
                    r, q = s % t, s // t
                    src = srcs[q + (1 if r > jo else 0)]
                    acc = acc + wk[width - 1 - s] * src[(jo - r) % t, pl.ds(r0, sl), c0:c0 + lw]
                cv_ref[jo, pl.ds(r0, sl), c0:c0 + lw] = acc
            return carry

        lax.fori_loop(0, nr // sl, row_tile, 0, unroll=4)
    n_parts = 4
    th = t // n_parts
    for part in range(n_parts):
        for jo in range(part * th, (part + 1) * th):
            acc = cv_ref[jo]
            mu = jnp.mean(acc, axis=-1, keepdims=True)
            xc = acc - mu
            var = jnp.mean(xc * xc, axis=-1, keepdims=True)
            yn = xc * lax.rsqrt(var + LN_EPS) * lng + lnb
            yc_ref[jo * nr:(jo + 1) * nr, :] = (_silu(yn) * zc_ref[jo].astype(F32)).astype(BF16)
        rows = slice(part * th * nr, (part + 1) * th * nr)
        slabs = slice(part * th, (part + 1) * th)
        pc = _dot(yc_ref[rows, :], wc_ref[...])
        gate = gc_ref[slabs].reshape(th * nr, d).astype(F32)
        o_ref[slabs] = (gate * pc).astype(BF16).reshape(th, nr, d)


def _ssm_out_kernel(yt_ref, zt_ref, mc_ref, gs_ref, wg_ref, bcol_ref, ws_ref, w_ref, gain_ref,
                    x_hbm, o_hbm, xbuf, obuf, sem_in, sem_out):
    j, r = pl.program_id(0), pl.program_id(1)
    nr = pl.num_programs(1)
    step, n_steps = j * nr + r, pl.num_programs(0) * nr
    _, hb, nc, d = xbuf.shape
    slot = step % 2

    def x_copy(jj, rr, sl):
        return pltpu.make_async_copy(x_hbm.at[pl.ds(rr * hb, hb), :, jj, :], xbuf.at[sl], sem_in.at[sl])

    def o_copy(jj, rr, sl):
        return pltpu.make_async_copy(obuf.at[sl], o_hbm.at[pl.ds(rr * hb, hb), :, jj, :], sem_out.at[sl])

    @pl.when(step == 0)
    def _():
        x_copy(0, 0, 0).start()

    x_copy(j, r, slot).wait()

    @pl.when(step + 1 < n_steps)
    def _():
        nxt = step + 1
        x_copy(nxt // nr, nxt % nr, 1 - slot).start()

    @pl.when(step >= 2)
    def _():
        o_copy(j, r, slot).wait()

    y = jax.nn.gelu(yt_ref[...].astype(F32), approximate=True)
    g = _dot(wg_ref[...], y.astype(BF16)) + bcol_ref[...]
    y2t = (y * _sigmoid(g) * zt_ref[...].astype(F32)).astype(BF16)
    ps = _dot_tn(y2t, ws_ref[...])
    merged = (mc_ref[...].astype(F32) + gs_ref[...].astype(F32) * ps).astype(BF16)
    hres = xbuf[slot].reshape(hb * nc, d) + _dot(merged, w_ref[...])
    ms = jnp.mean(hres * hres, axis=-1, keepdims=True)
    obuf[slot] = (hres * lax.rsqrt(ms + NORM_EPS) * gain_ref[...]).reshape(hb, nc, d)
    o_copy(j, r, slot).start()

    @pl.when(step == n_steps - 1)
    def _():
        o_copy(j, r, 1 - slot).wait()
        o_copy(j, r, slot).wait()


def _col_tile(n):
    return 512 if n % 512 == 0 else n


def kernel(x, meta, norm_g, w_in, b_gate, dw_w, dw_b, ln_g, ln_b, w_conv, lam_re, lam_im,
           log_step, b_re, b_im, c_re, c_im, d_skip, w_glu, b_glu, w_ssm, w_out, final_g):
    bsz, seq, d = x.shape
    t = T_CHUNK
    e = w_conv.shape[0]
    g_n, p_n = lam_re.shape
    h_n = e // g_n
    width = dw_w.shape[0]
    nc = seq // t
    rows = bsz * nc
    assert meta.shape[0] == t and h_n * t == 2 * LANES_V7X and nc == LANES_V7X
    assert width <= 2 * t - 1 and seq % t == 0 and 2 * p_n == LANES_V7X and g_n % 2 == 0
    tn = _col_tile(e)
    ne = e // tn

    x4 = x.reshape(bsz, nc, t, d)
    gain = norm_g.reshape(1, d)

    assert 2 * d == N_GATE_BLOCKS * e and w_in.shape[1] == PROJ_STEPS * e and PROJ_STEPS % N_WSLOTS == 0
    hbm = pl.BlockSpec(memory_space=pl.ANY)
    a_slab, zc_slab, gates, ut, zst, w_in_b = pl.pallas_call(
        _proj_kernel,
        grid=(t,),
        in_specs=[hbm, hbm,
                  pl.BlockSpec((1, d), lambda j: (0, 0)),
                  pl.BlockSpec((1, 2 * d), lambda j: (0, 0))],
        out_specs=[hbm] * 6,
        out_shape=[jax.ShapeDtypeStruct((t, rows, e), F32),
                   jax.ShapeDtypeStruct((t, rows, e), BF16),
                   jax.ShapeDtypeStruct((t, rows, 2 * d), BF16),
                   jax.ShapeDtypeStruct((g_n, t * h_n, rows), BF16),
                   jax.ShapeDtypeStruct((t, e, rows), BF16),
                   jax.ShapeDtypeStruct(w_in.shape, BF16)],
        scratch_shapes=[pltpu.VMEM((bsz, nc, d), F32), pltpu.VMEM((rows, d), BF16),
                        pltpu.VMEM((N_WSLOTS, d, e), BF16), pltpu.VMEM((d, e), F32),
                        pltpu.VMEM((rows, e), F32), pltpu.VMEM((rows, e), BF16),
                        pltpu.VMEM((2, rows, e), BF16), pltpu.VMEM((g_n, h_n, rows), BF16),
                        pltpu.VMEM((e, rows), BF16),
                        pltpu.SemaphoreType.DMA((1,)), pltpu.SemaphoreType.DMA((N_WSLOTS,)),
                        pltpu.SemaphoreType.DMA((1,)), pltpu.SemaphoreType.DMA((N_WSLOTS,)),
                        pltpu.SemaphoreType.DMA((6,))],
        compiler_params=_cparams("arbitrary"),
        name="proj",
    )(x4, w_in, gain, b_gate.reshape(1, 2 * d))

    a_meta, u_meta = pl.pallas_call(
        _proj_meta_kernel,
        grid=(ne,),
        in_specs=[pl.BlockSpec((t, d), lambda n: (0, 0)), pl.BlockSpec((1, d), lambda n: (0, 0)),
                  pl.BlockSpec((d, tn), lambda n: (0, n)), pl.BlockSpec((d, tn), lambda n: (0, ne + n)),
                  pl.BlockSpec((d, tn), lambda n: (0, 3 * ne + n))],
        out_specs=[pl.BlockSpec((t, tn), lambda n: (0, n)), pl.BlockSpec((t, tn), lambda n: (0, n))],
        out_shape=[jax.ShapeDtypeStruct((t, e), F32), jax.ShapeDtypeStruct((t, e), F32)],
        compiler_params=_cparams("arbitrary"),
        name="proj_meta",
    )(meta, gain, w_in_b, w_in_b, w_in_b)
    um_col = u_meta.reshape(t, g_n, h_n).transpose(1, 0, 2).reshape(g_n, t * h_n, 1)

    grp = lambda *shape: pl.BlockSpec((None,) + shape, lambda g: (g,) + (0,) * len(shape))
    tile2 = lambda v: jnp.concatenate([v, v], axis=-1)
    gpp = 8
    gblk = lambda *shape: pl.BlockSpec((gpp,) + shape, lambda g: (g,) + (0,) * len(shape))
    m_op, r_op, o_op, a_pow = pl.pallas_call(
        _s5_param_kernel,
        grid=(g_n // gpp,),
        in_specs=[gblk(p_n, 1), gblk(p_n, 1), gblk(1, 2 * p_n), gblk(1, 2 * p_n), gblk(1, 1),
                  gblk(p_n, h_n), gblk(p_n, h_n), gblk(h_n, 2 * p_n), gblk(h_n, 2 * p_n)],
        out_specs=[gblk(t * h_n, t * h_n), gblk(2 * p_n, t * h_n), gblk(t * h_n, 2 * p_n),
                   gblk(2 * SUBLANES_V7X, 2 * p_n)],
        out_shape=[jax.ShapeDtypeStruct((g_n, t * h_n, t * h_n), BF16),
                   jax.ShapeDtypeStruct((g_n, 2 * p_n, t * h_n), BF16),
                   jax.ShapeDtypeStruct((g_n, t * h_n, 2 * p_n), BF16),
                   jax.ShapeDtypeStruct((g_n, 2 * SUBLANES_V7X, 2 * p_n), F32)],
        compiler_params=_cparams("parallel"),
        name="s5_params",
    )(lam_re.reshape(g_n, p_n, 1), lam_im.reshape(g_n, p_n, 1),
      tile2(lam_re).reshape(g_n, 1, 2 * p_n), tile2(lam_im).reshape(g_n, 1, 2 * p_n),
      log_step.reshape(g_n, 1, 1),
      b_re, b_im,
      jnp.concatenate([c_re, c_im], axis=-1), jnp.concatenate([c_im, c_re], axis=-1))
    gps = 4
    pw_pair = (a_pow[:, :, :p_n].reshape(g_n // gps, gps, 2 * SUBLANES_V7X, p_n)
               .transpose(0, 2, 1, 3).reshape(g_n // gps, 2 * SUBLANES_V7X, gps * p_n))

    dcol = jnp.tile(d_skip.reshape(g_n, 1, h_n), (1, t, 1)).reshape(g_n, t * h_n, 1)
    pair = lambda *shape: pl.BlockSpec((gps,) + shape, lambda g: (g,) + (0,) * len(shape))
    yt = pl.pallas_call(
        _s5_scan_kernel,
        grid=(g_n // gps,),
        in_specs=[pair(t * h_n, rows), pair(t * h_n, 1), pair(t * h_n, t * h_n), pair(2 * p_n, t * h_n),
                  pair(t * h_n, 2 * p_n), grp(2 * SUBLANES_V7X, gps * p_n), pair(t * h_n, 1)],
        out_specs=pl.BlockSpec((t, gps * h_n, rows), lambda g: (0, g, 0)),
        out_shape=jax.ShapeDtypeStruct((t, e, rows), BF16),
        compiler_params=_cparams("parallel"),
        name="s5_scan",
    )(ut, um_col, m_op, r_op, o_op, pw_pair, dcol)

    nr = 64
    hr = SUBLANES_V7X
    ranges_per_seq = nc // nr
    mc = pl.pallas_call(
        functools.partial(_conv_branch_kernel, width=width, ranges_per_seq=ranges_per_seq),
        grid=(rows // nr,),
        in_specs=[pl.BlockSpec((t, nr, e), lambda q: (0, q, 0)),
                  pl.BlockSpec((t, hr, e), lambda q: (0, jnp.maximum(q * (nr // hr) - 1, 0), 0)),
                  pl.BlockSpec((t, e), lambda q: (0, 0)),
                  pl.BlockSpec((t, nr, e), lambda q: (0, q, 0)),
                  pl.BlockSpec((t, nr, d), lambda q: (0, q, 0)),
                  pl.BlockSpec((width, SUBLANES_V7X, e), lambda q: (0, 0, 0)),
                  pl.BlockSpec((1, e), lambda q: (0, 0)),
                  pl.BlockSpec((1, e), lambda q: (0, 0)),
                  pl.BlockSpec((1, e), lambda q: (0, 0)),
                  pl.BlockSpec((e, d), lambda q: (0, 0))],
        out_specs=pl.BlockSpec((t, nr, d), lambda q: (0, q, 0)),
        out_shape=jax.ShapeDtypeStruct((t, rows, d), BF16),
        scratch_shapes=[pltpu.VMEM((t, nr, e), F32), pltpu.VMEM((t, nr, e), F32),
                        pltpu.VMEM((t * nr, e), BF16), pltpu.VMEM((t, nr, e), F32)],
        compiler_params=_cparams("parallel"),
        name="conv_branch",
    )(a_slab, a_slab, a_meta, zc_slab, gates,
      jnp.broadcast_to(dw_w.reshape(width, 1, e), (width, SUBLANES_V7X, e)), dw_b.reshape(1, e),
      ln_g.reshape(1, e), ln_b.reshape(1, e), w_conv.astype(BF16))

    rt = rows // 2
    hb = bsz // 2
    const = lambda *shape: pl.BlockSpec(shape, lambda j, r: (0,) * len(shape), pipeline_mode=pl.Buffered(1))
    out = pl.pallas_call(
        _ssm_out_kernel,
        grid=(t, 2),
        in_specs=[pl.BlockSpec((None, e, rt), lambda j, r: (j, 0, r)),
                  pl.BlockSpec((None, e, rt), lambda j, r: (j, 0, r)),
                  pl.BlockSpec((None, rt, d), lambda j, r: (j, r, 0)),
                  pl.BlockSpec((None, rt, d), lambda j, r: (j, r, 1)),
                  const(e, e), const(e, 1), const(e, d), const(d, d), const(1, d),
                  pl.BlockSpec(memory_space=pl.ANY)],
        out_specs=pl.BlockSpec(memory_space=pl.ANY),
        out_shape=jax.ShapeDtypeStruct((bsz, nc, t, d), F32),
        scratch_shapes=[pltpu.VMEM((2, hb, nc, d), F32), pltpu.VMEM((2, hb, nc, d), F32),
                        pltpu.SemaphoreType.DMA((2,)), pltpu.SemaphoreType.DMA((2,))],
        compiler_params=_cparams("arbitrary", "arbitrary"),
        name="ssm_out",
    )(yt, zst, mc, gates, w_glu.T.astype(BF16), b_glu.reshape(e, 1), w_ssm.astype(BF16),
      w_out.astype(BF16), final_g.reshape(1, d), x4)
    return out.reshape(bsz, seq, d)
```
